```python
import math
import numpy as np
import jax
import jax.numpy as jnp
from jax import lax

D_MODEL = 1024
BATCH = 16
SEQ = 2048
DEPTH = 2

NORM_EPS = 1e-6
GLA_HEADS = 4
GLA_DV = D_MODEL // GLA_HEADS
GLA_DK = GLA_DV // 2
GLA_QK = GLA_HEADS * GLA_DK
GLA_GATE_RANK = 16
GLA_GATE_NORM = 16.0
GLA_CHUNK = 16
MLSTM_HEADS = 4
MLSTM_DV = D_MODEL // MLSTM_HEADS
MLSTM_DK = MLSTM_DV // 2
MLSTM_QK = MLSTM_HEADS * MLSTM_DK
MLSTM_CHUNK = 64
MLSTM_GATE_CAP = 15.0
MLSTM_M_INIT = -1e30
GDN_HEADS = 8
GDN_DK = D_MODEL // GDN_HEADS
GDN_DV = D_MODEL // GDN_HEADS
GDN_QKV = GDN_HEADS * (2 * GDN_DK + GDN_DV)
GDN_CONV = 4
GDN_CHUNK = 64
N_BRANCHES = 3
D_FF = ((8 * D_MODEL + 3 * 256 - 1) // (3 * 256)) * 256
IN_SIZES = (GLA_QK, GLA_QK, D_MODEL, D_MODEL, GLA_GATE_RANK,
            MLSTM_QK, MLSTM_QK, D_MODEL, D_MODEL, MLSTM_HEADS, MLSTM_HEADS,
            GDN_QKV, D_MODEL, GDN_HEADS, GDN_HEADS,
            N_BRANCHES * D_MODEL)
D_IN = sum(IN_SIZES)

kernel_name = 'hybrid_gla_mlstm_gdn_adaln_block'


def rmsnorm(x, gain):
    xf = x.astype(jnp.float32)
    y = xf * lax.rsqrt(jnp.mean(xf * xf, axis=-1, keepdims=True) + NORM_EPS)
    return (y * gain.astype(jnp.float32)).astype(x.dtype)


def to_chunks(t, n_heads, head_dim, chunk):
    b, s = t.shape[0], t.shape[1]
    t = t.reshape(b, s // chunk, chunk, n_heads, head_dim)
    return t.transpose(0, 3, 1, 2, 4).astype(jnp.float32)


def gate_chunks(t, chunk):
    b, s, h = t.shape
    return t.reshape(b, s // chunk, chunk, h).transpose(0, 3, 1, 2).astype(jnp.float32)


def from_chunks(o):
    b, h, n, c, d = o.shape
    return o.transpose(0, 2, 3, 1, 4).reshape(b, n * c, h, d)


def chunk_major(*ts):
    return tuple(jnp.moveaxis(t, 2, 0) for t in ts)


def causal_mask(c, strict=False):
    return jnp.tril(jnp.ones((c, c), dtype=bool), -1 if strict else 0)


def softcap(t, cap):
    return cap * jnp.tanh(t / cap)


def l2norm(t):
    return t * lax.rsqrt(jnp.sum(t * t, axis=-1, keepdims=True) + NORM_EPS)


def causal_depthwise_conv(t, w):
    return lax.conv_general_dilated(
        t, w[:, None, :], window_strides=(1,), padding=[(w.shape[0] - 1, 0)],
        dimension_numbers=('NWC', 'WIO', 'NWC'), feature_group_count=t.shape[-1])


def gla_mixer(q, k, v, r, g_lr, w_g2, b_g2, norm_gain):
    b, s, _ = q.shape
    f32 = jnp.float32
    log_alpha = jax.nn.log_sigmoid(g_lr.astype(f32) @ w_g2.astype(f32) + b_g2.astype(f32)) / GLA_GATE_NORM
    qc = to_chunks(q, GLA_HEADS, GLA_DK, GLA_CHUNK) * (GLA_DK ** -0.5)
    kc = to_chunks(k, GLA_HEADS, GLA_DK, GLA_CHUNK)
    vc = to_chunks(v, GLA_HEADS, GLA_DV, GLA_CHUNK)
    cum = jnp.cumsum(to_chunks(log_alpha, GLA_HEADS, GLA_DK, GLA_CHUNK), axis=3)
    q_dec = qc * jnp.exp(cum)
    k_inv = kc * jnp.exp(-cum)
    k_end = kc * jnp.exp(cum[..., -1:, :] - cum)
    decay_last = jnp.exp(cum[..., -1, :])
    scores = jnp.einsum('bhntd,bhnsd->bhnts', q_dec, k_inv)
    scores = jnp.where(causal_mask(GLA_CHUNK), scores, 0.0)
    o_intra = jnp.einsum('bhnts,bhnsv->bhntv', scores, vc)

    def step(state, inp):
        qd, ke, vv, dl = inp
        o = jnp.einsum('bhtk,bhkv->bhtv', qd, state)
        state = state * dl[..., None] + jnp.einsum('bhsk,bhsv->bhkv', ke, vv)
        return state, o

    s0 = jnp.zeros((b, GLA_HEADS, GLA_DK, GLA_DV), f32)
    _, o_inter = lax.scan(step, s0, chunk_major(q_dec, k_end, vc, decay_last))
    o = o_intra + jnp.moveaxis(o_inter, 0, 2)
    o = rmsnorm(from_chunks(o), norm_gain).reshape(b, s, D_MODEL)
    return o * jax.nn.silu(r.astype(f32))


def mlstm_mixer(q, k, v, o_raw, i_raw, f_raw, i_bias, f_bias, norm_gain):
    b, s, _ = q.shape
    f32 = jnp.float32
    c = MLSTM_CHUNK
    qc = to_chunks(q, MLSTM_HEADS, MLSTM_DK, c) * (MLSTM_DK ** -0.5)
    kc = to_chunks(k, MLSTM_HEADS, MLSTM_DK, c)
    vc = to_chunks(v, MLSTM_HEADS, MLSTM_DV, c)
    log_i = gate_chunks(softcap(i_raw.astype(f32) + i_bias.astype(f32), MLSTM_GATE_CAP), c)
    log_f = jax.nn.log_sigmoid(gate_chunks(softcap(f_raw.astype(f32) + f_bias.astype(f32), MLSTM_GATE_CAP), c))
    cum = jnp.cumsum(log_f, axis=-1)
    log_d = cum[..., :, None] - cum[..., None, :] + log_i[..., None, :]
    log_d = jnp.where(causal_mask(c), log_d, -jnp.inf)
    m_intra = jnp.max(log_d, axis=-1)
    sc = jnp.einsum('bhntd,bhnsd->bhnts', qc, kc) * jnp.exp(log_d - m_intra[..., None])
    num_intra = jnp.einsum('bhnts,bhnsv->bhntv', sc, vc)
    den_intra = jnp.sum(sc, axis=-1)
    log_end = cum[..., -1:] - cum + log_i
    m_end = jnp.max(log_end, axis=-1)
    cum_last = cum[..., -1]

    def step(carry, inp):
        cs, ns, m = carry
        qq, kk, vv, cm, mi, numi, deni, le, me, cl = inp
        m_inter = cm + m[..., None]
        m_t = jnp.maximum(m_inter, mi)
        a_inter = jnp.exp(m_inter - m_t)
        a_intra = jnp.exp(mi - m_t)
        num = a_intra[..., None] * numi + a_inter[..., None] * jnp.einsum('bhtk,bhkv->bhtv', qq, cs)
        den = a_intra * deni + a_inter * jnp.einsum('bhtk,bhk->bht', qq, ns)
        h = num / jnp.maximum(jnp.abs(den), jnp.exp(-m_t))[..., None]
        m_new = jnp.maximum(cl + m, me)
        w = jnp.exp(le - m_new[..., None])
        dec = jnp.exp(cl + m - m_new)
        kw = kk * w[..., None]
        cs = dec[..., None, None] * cs + jnp.einsum('bhsk,bhsv->bhkv', kw, vv)
        ns = dec[..., None] * ns + jnp.sum(kw, axis=2)
        return (cs, ns, m_new), h

    init = (jnp.zeros((b, MLSTM_HEADS, MLSTM_DK, MLSTM_DV), f32),
            jnp.zeros((b, MLSTM_HEADS, MLSTM_DK), f32),
            jnp.full((b, MLSTM_HEADS), MLSTM_M_INIT, f32))
    xs = chunk_major(qc, kc, vc, cum, m_intra, num_intra, den_intra, log_end, m_end, cum_last)
    _, h = lax.scan(step, init, xs)
    h = rmsnorm(from_chunks(jnp.moveaxis(h, 0, 2)), norm_gain).reshape(b, s, D_MODEL)
    return h * jax.nn.sigmoid(o_raw.astype(f32))


def gdn_mixer(qkv_raw, z, a_raw, b_raw, conv_w, a_log, dt_bias, norm_gain):
    b, s, _ = qkv_raw.shape
    f32 = jnp.float32
    c = GDN_CHUNK
    qkv = jax.nn.silu(causal_depthwise_conv(qkv_raw.astype(f32), conv_w.astype(f32)))
    q, k, v = jnp.split(qkv, [GDN_HEADS * GDN_DK, 2 * GDN_HEADS * GDN_DK], axis=-1)
    qc = l2norm(to_chunks(q, GDN_HEADS, GDN_DK, c)) * (GDN_DK ** -0.5)
    kc = l2norm(to_chunks(k, GDN_HEADS, GDN_DK, c))
    vc = to_chunks(v, GDN_HEADS, GDN_DV, c)
    beta = gate_chunks(jax.nn.sigmoid(b_raw.astype(f32)), c)
    g = -jnp.exp(a_log.astype(f32)) * jax.nn.softplus(a_raw.astype(f32) + dt_bias.astype(f32))
    cum = jnp.cumsum(gate_chunks(g, c), axis=-1)
    gamma = jnp.exp(jnp.where(causal_mask(c), cum[..., :, None] - cum[..., None, :], -jnp.inf))
    kk = jnp.einsum('bhnid,bhnjd->bhnij', kc, kc)
    a_mat = jnp.where(causal_mask(c, strict=True), beta[..., :, None] * kk * gamma, 0.0)
    t_mat = a_mat + jnp.eye(c, dtype=f32)
    u = lax.linalg.triangular_solve(t_mat, vc * beta[..., None], left_side=True, lower=True, unit_diagonal=True)
    w = lax.linalg.triangular_solve(t_mat, kc * (beta * jnp.exp(cum))[..., None], left_side=True, lower=True, unit_diagonal=True)
    a_qk = jnp.einsum('bhnid,bhnjd->bhnij', qc, kc) * gamma
    q_dec = qc * jnp.exp(cum)[..., None]
    k_end = kc * jnp.exp(cum[..., -1:] - cum)[..., None]
    decay_last = jnp.exp(cum[..., -1])

    def step(state, inp):
        qd, ke, ww, uu, aq, dl = inp
        v_new = uu - jnp.einsum('bhck,bhkv->bhcv', ww, state)
        o = jnp.einsum('bhck,bhkv->bhcv', qd, state) + jnp.einsum('bhts,bhsv->bhtv', aq, v_new)
        state = state * dl[..., None, None] + jnp.einsum('bhsk,bhsv->bhkv', ke, v_new)
        return state, o

    s0 = jnp.zeros((b, GDN_HEADS, GDN_DK, GDN_DV), f32)
    _, o = lax.scan(step, s0, chunk_major(q_dec, k_end, w, u, a_qk, decay_last))
    o = rmsnorm(from_chunks(jnp.moveaxis(o, 0, 2)), norm_gain)
    o = o * jax.nn.silu(z.astype(f32).reshape(b, s, GDN_HEADS, GDN_DV))
    return o.reshape(b, s, D_MODEL)


def hybrid_mixer(h, w_in, gla_w_g2, gla_b_g2, gla_norm, ml_i_bias, ml_f_bias, ml_norm,
                 gdn_conv, gdn_a_log, gdn_dt_bias, gdn_norm, w_out):
    b, s, _ = h.shape
    proj = h @ w_in
    offsets = [int(o) for o in np.cumsum(IN_SIZES)[:-1]]
    (gla_q, gla_k, gla_v, gla_r, gla_glr,
     ml_q, ml_k, ml_v, ml_o, ml_i, ml_f,
     gdn_qkv, gdn_z, gdn_a, gdn_b, merge) = jnp.split(proj, offsets, axis=-1)
    y_gla = gla_mixer(gla_q, gla_k, gla_v, gla_r, gla_glr, gla_w_g2, gla_b_g2, gla_norm)
    y_ml = mlstm_mixer(ml_q, ml_k, ml_v, ml_o, ml_i, ml_f, ml_i_bias, ml_f_bias, ml_norm)
    y_gdn = gdn_mixer(gdn_qkv, gdn_z, gdn_a, gdn_b, gdn_conv, gdn_a_log, gdn_dt_bias, gdn_norm)
    gates = jax.nn.sigmoid(merge.astype(jnp.float32)).reshape(b, s, N_BRANCHES, D_MODEL)
    y = gates[:, :, 0] * y_gla + gates[:, :, 1] * y_ml + gates[:, :, 2] * y_gdn
    return (y @ w_out.astype(jnp.float32)).astype(h.dtype)


def swiglu(h, w_up, w_down):
    gate, val = jnp.split(h @ w_up, 2, axis=-1)
    return (jax.nn.silu(gate) * val) @ w_down


def setup_inputs(seed: int = 0) -> dict:
    key = jax.random.key(seed)
    ks = jax.random.split(key, 24)
    f32 = jnp.float32
    L = DEPTH

    def nrm(k, shape, scale):
        return jax.random.normal(k, shape, f32) * scale

    dt = jnp.exp(jax.random.uniform(ks[15], (L, GDN_HEADS), f32, math.log(1e-3), math.log(0.1)))
    return {
        'x': nrm(ks[0], (BATCH, SEQ, D_MODEL), 1.0),
        'c': nrm(ks[1], (BATCH, D_MODEL), 1.0),
        'ada_w': nrm(ks[2], (L, D_MODEL, 6 * D_MODEL), 0.5 * D_MODEL ** -0.5),
        'ada_b': nrm(ks[3], (L, 6 * D_MODEL), 0.02),
        'norm_mix': 1.0 + nrm(ks[4], (L, D_MODEL), 0.05),
        'norm_ffn': 1.0 + nrm(ks[5], (L, D_MODEL), 0.05),
        'w_in': nrm(ks[6], (L, D_MODEL, D_IN), D_MODEL ** -0.5),
        'gla_w_g2': nrm(ks[7], (L, GLA_GATE_RANK, GLA_QK), GLA_GATE_RANK ** -0.5),
        'gla_b_g2': nrm(ks[8], (L, GLA_QK), 0.1),
        'gla_norm': 1.0 + nrm(ks[9], (L, GLA_DV), 0.05),
        'ml_i_bias': nrm(ks[10], (L, MLSTM_HEADS), 0.1),
        'ml_f_bias': jnp.linspace(3.0, 6.0, MLSTM_HEADS, dtype=f32) + nrm(ks[11], (L, MLSTM_HEADS), 0.1),
        'ml_norm': 1.0 + nrm(ks[12], (L, MLSTM_DV), 0.05),
        'gdn_conv': nrm(ks[13], (L, GDN_CONV, GDN_QKV), GDN_CONV ** -0.5),
        'gdn_a_log': jnp.log(jax.random.uniform(ks[14], (L, GDN_HEADS), f32, 1.0, 16.0)),
        'gdn_dt_bias': dt + jnp.log(-jnp.expm1(-dt)),
        'gdn_norm': 1.0 + nrm(ks[16], (L, GDN_DV), 0.05),
        'w_out': nrm(ks[17], (L, D_MODEL, D_MODEL), D_MODEL ** -0.5),
        'w_ffn_up': nrm(ks[18], (L, D_MODEL, 2 * D_FF), D_MODEL ** -0.5),
        'w_ffn_down': nrm(ks[19], (L, D_FF, D_MODEL), D_FF ** -0.5),
        'norm_final': 1.0 + nrm(ks[20], (D_MODEL,), 0.05),
    }


def reference(x, c, ada_w, ada_b, norm_mix, norm_ffn, w_in, gla_w_g2, gla_b_g2, gla_norm,
              ml_i_bias, ml_f_bias, ml_norm, gdn_conv, gdn_a_log, gdn_dt_bias, gdn_norm,
              w_out, w_ffn_up, w_ffn_down, norm_final):
    c_act = jax.nn.silu(c)
    for l in range(DEPTH):
        mod = (c_act @ ada_w[l] + ada_b[l])[:, None, :]
        sh_mix, sc_mix, gt_mix, sh_ffn, sc_ffn, gt_ffn = jnp.split(mod, 6, axis=-1)
        h = rmsnorm(x, norm_mix[l]) * (1.0 + sc_mix) + sh_mix
        x = x + gt_mix * hybrid_mixer(h, w_in[l], gla_w_g2[l], gla_b_g2[l], gla_norm[l],
                                      ml_i_bias[l], ml_f_bias[l], ml_norm[l],
                                      gdn_conv[l], gdn_a_log[l], gdn_dt_bias[l], gdn_norm[l], w_out[l])
        h = rmsnorm(x, norm_ffn[l]) * (1.0 + sc_ffn) + sh_ffn
        x = x + gt_ffn * swiglu(h, w_ffn_up[l], w_ffn_down[l])
    return rmsnorm(x, norm_final)
```

```python
import functools

import jax
import jax.numpy as jnp
from jax import lax
from jax.experimental import pallas as pl
from jax.experimental.pallas import tpu as pltpu

F32 = jnp.float32
BF16 = jnp.bfloat16

D_MODEL = 1024
NORM_EPS = 1e-6
GLA_HEADS, GLA_DK, GLA_DV = 4, 128, 256
GLA_GATE_RANK = 16
GLA_GATE_NORM = 16.0
GLA_SUB = 16
ML_HEADS, ML_DK, ML_DV = 4, 128, 256
ML_GATE_CAP = 15.0
ML_M_INIT = -1e30
GDN_HEADS, GDN_DK, GDN_DV = 8, 128, 128
GDN_CONV = 4
GDN_QKV = GDN_HEADS * (2 * GDN_DK + GDN_DV)
D_FF = 2816
FF_CHUNK = 256

OFF_GDN_QKV = 0
OFF_MERGE = 3072
OFF_GLA_V = 6144
OFF_GLA_R = 7168
OFF_ML_V = 8192
OFF_ML_O = 9216
OFF_GDN_Z = 10240
OFF_GLA_Q = 11264
OFF_GLA_K = 11776
OFF_ML_Q = 12288
OFF_ML_K = 12800
N_MAIN = 13312
N_GATE = 128
LANE_GLR = 0
LANE_ML_I = 16
LANE_ML_F = 20
LANE_GDN_A = 24
LANE_GDN_B = 32

LANES = 128
SUBLANES = 8
VMEM_LIMIT = 56 * 1024 * 1024

MIX_BLOCK = 128
PROJ_TM = 1024
PROJ_TN = 1024
TOK_TM = 512


def _dot(a, b):
    return jnp.dot(a, b, preferred_element_type=F32)


def _dot_nt(a, b):
    return lax.dot_general(a, b, (((1,), (1,)), ((), ())), preferred_element_type=F32)


def _dot_tn(a, b):
    return lax.dot_general(a, b, (((0,), (0,)), ((), ())), preferred_element_type=F32)


def _split3(x):
    hi = x.astype(BF16)
    r = x - hi.astype(F32)
    mid = r.astype(BF16)
    lo = (r - mid.astype(F32)).astype(BF16)
    return hi, mid, lo


def _dot_f32(a, b):
    a0, a1, a2 = _split3(a)
    b0, b1, b2 = _split3(b)
    small = _dot(a0, b2) + _dot(a2, b0) + _dot(a1, b1)
    return _dot(a0, b0) + (_dot(a0, b1) + _dot(a1, b0) + small)


def _cumsum_rows(tri, x):
    hi, mid, lo = _split3(x)
    return _dot(tri, hi) + (_dot(tri, mid) + _dot(tri, lo))


def _log_sigmoid(x):
    return jnp.minimum(x, 0.0) - jnp.log1p(jnp.exp(-jnp.abs(x)))


def _softplus(x):
    return jnp.maximum(x, 0.0) + jnp.log1p(jnp.exp(-jnp.abs(x)))


def _sigmoid(x):
    return 1.0 / (1.0 + jnp.exp(-x))


def _silu(x):
    return x * _sigmoid(x)


def _rms(x, gain):
    return x * lax.rsqrt(jnp.mean(x * x, axis=-1, keepdims=True) + NORM_EPS) * gain


def _params(*sem):
    return pltpu.CompilerParams(dimension_semantics=sem, vmem_limit_bytes=VMEM_LIMIT)


def _adaln_kernel(c_ref, w_ref, b_ref, o_ref):
    c = c_ref[...]
    o_ref[...] = _dot_f32(_silu(c), w_ref[...]) + b_ref[...]


def _adaln(c, ada_w, ada_b):
    depth, d, n = ada_w.shape
    b = c.shape[0]
    tn = D_MODEL
    return pl.pallas_call(
        _adaln_kernel,
        grid=(depth, n // tn),
        in_specs=[
            pl.BlockSpec((b, d), lambda l, j: (0, 0)),
            pl.BlockSpec((None, d, tn), lambda l, j: (l, 0, j)),
            pl.BlockSpec((None, 1, tn), lambda l, j: (l, 0, j)),
        ],
        out_specs=pl.BlockSpec((None, b, tn), lambda l, j: (l, 0, j)),
        out_shape=jax.ShapeDtypeStruct((depth, b, n), F32),
        compiler_params=_params("arbitrary", "arbitrary"),
        name="adaln",
    )(c, ada_w, ada_b.reshape(depth, 1, n))


def _inproj_kernel(x_ref, mod_ref, gain_ref, w_ref, wg_ref, o_ref, og_ref, h_ref):
    @pl.when(pl.program_id(1) == 0)
    def _():
        h = _rms(x_ref[...], gain_ref[...]) * (1.0 + mod_ref[1:2, :]) + mod_ref[0:1, :]
        hb = h.astype(BF16)
        h_ref[...] = hb
        og_ref[...] = _dot(hb, wg_ref[...])

    o_ref[...] = _dot(h_ref[...], w_ref[...]).astype(BF16)


def _inproj(x2, mod, gain, w_main, w_gate, seq):
    t, d = x2.shape
    tm = min(PROJ_TM, seq)
    per_seq = seq // tm
    return pl.pallas_call(
        _inproj_kernel,
        grid=(t // tm, N_MAIN // PROJ_TN),
        in_specs=[
            pl.BlockSpec((tm, d), lambda i, j: (i, 0)),
            pl.BlockSpec((None, 6, d), lambda i, j: (i // per_seq, 0, 0)),
            pl.BlockSpec((1, d), lambda i, j: (0, 0)),
            pl.BlockSpec((d, PROJ_TN), lambda i, j: (0, j)),
            pl.BlockSpec((d, N_GATE), lambda i, j: (0, 0)),
        ],
        out_specs=[
            pl.BlockSpec((tm, PROJ_TN), lambda i, j: (i, j)),
            pl.BlockSpec((tm, N_GATE), lambda i, j: (i, 0)),
        ],
        out_shape=[
            jax.ShapeDtypeStruct((t, N_MAIN), BF16),
            jax.ShapeDtypeStruct((t, N_GATE), F32),
        ],
        scratch_shapes=[pltpu.VMEM((tm, d), BF16)],
        compiler_params=_params("arbitrary", "arbitrary"),
        name="inproj",
    )(x2, mod, gain, w_main, w_gate)


def _row_refs(c, block, seg, pick):
    parts = []
    for start in range(0, block, seg):
        r = pick(start)
        row = jnp.zeros((1, c.shape[1]), F32) if r is None else c[r:r + 1, :]
        parts.append(jnp.broadcast_to(row, (seg, c.shape[1])))
    return parts[0] if len(parts) == 1 else jnp.concatenate(parts, axis=0)


def _gla_kernel(q_ref, k_ref, v_ref, r_ref, g_ref, wg_ref, bg_ref, gain_ref, o_ref, st_ref, *, block):
    @pl.when(pl.program_id(1) == 0)
    def _():
        st_ref[...] = jnp.zeros_like(st_ref)

    row = lax.broadcasted_iota(jnp.int32, (block, block), 0)
    col = lax.broadcasted_iota(jnp.int32, (block, block), 1)
    tri = (col <= row).astype(BF16)
    sub_xor = (row // GLA_SUB) ^ (col // GLA_SUB)
    causal = col <= row

    g = g_ref[...]
    g0, g1, _ = _split3(g)
    w0, w1, _ = _split3(wg_ref[...])
    x = _dot(g0, w0) + (_dot(g0, w1) + _dot(g1, w0)) + bg_ref[...]
    log_alpha = _log_sigmoid(x) * (1.0 / GLA_GATE_NORM)
    cum = _cumsum_rows(tri, log_alpha)

    n_levels = (block // GLA_SUB).bit_length() - 1
    gain = gain_ref[...]
    for h in range(GLA_HEADS):
        c = cum[:, h * GLA_DK:(h + 1) * GLA_DK]
        q = q_ref[:, h * GLA_DK:(h + 1) * GLA_DK].astype(F32) * (GLA_DK ** -0.5)
        k = k_ref[:, h * GLA_DK:(h + 1) * GLA_DK].astype(F32)
        v = v_ref[:, h * GLA_DV:(h + 1) * GLA_DV]

        ref0 = _row_refs(c, block, GLA_SUB, lambda s: None if s == 0 else s - 1)
        p = _dot_nt((q * jnp.exp(c - ref0)).astype(BF16), (k * jnp.exp(ref0 - c)).astype(BF16))
        a = jnp.where(causal & (sub_xor == 0), p, 0.0)
        for lev in range(1, n_levels + 1):
            half = GLA_SUB << (lev - 1)
            ref = _row_refs(c, block, 2 * half, lambda s: s + half - 1)
            qs = q * jnp.exp(jnp.minimum(c - ref, 0.0))
            ks = k * jnp.exp(jnp.minimum(ref - c, 0.0))
            p = _dot_nt(qs.astype(BF16), ks.astype(BF16))
            lo, hi = 1 << (lev - 1), 1 << lev
            a = jnp.where(causal & (sub_xor >= lo) & (sub_xor < hi), p, a)

        st = st_ref[h]
        o = _dot(a.astype(BF16), v) + _dot_nt((q * jnp.exp(c)).astype(BF16), st.astype(BF16))
        c_end = c[block - 1:block, :]
        k_end = (k * jnp.exp(c_end - c)).astype(BF16)
        st_ref[h] = st * jnp.exp(c_end) + _dot_tn(v, k_end)
        r = r_ref[:, h * GLA_DV:(h + 1) * GLA_DV].astype(F32)
        o_ref[:, h * GLA_DV:(h + 1) * GLA_DV] = (_rms(o, gain) * _silu(r)).astype(o_ref.dtype)


def _gla(proj, gates, w_g2, b_g2, norm_gain):
    b, s, _ = proj.shape
    blk = min(MIX_BLOCK, s)
    qk_w = GLA_HEADS * GLA_DK
    w_g2p = jnp.zeros((N_GATE, qk_w), F32).at[LANE_GLR:LANE_GLR + GLA_GATE_RANK].set(w_g2)
    return pl.pallas_call(
        functools.partial(_gla_kernel, block=blk),
        grid=(b, s // blk),
        in_specs=[
            pl.BlockSpec((None, blk, qk_w), lambda i, t: (i, t, OFF_GLA_Q // qk_w)),
            pl.BlockSpec((None, blk, qk_w), lambda i, t: (i, t, OFF_GLA_K // qk_w)),
            pl.BlockSpec((None, blk, D_MODEL), lambda i, t: (i, t, OFF_GLA_V // D_MODEL)),
            pl.BlockSpec((None, blk, D_MODEL), lambda i, t: (i, t, OFF_GLA_R // D_MODEL)),
            pl.BlockSpec((None, blk, N_GATE), lambda i, t: (i, t, 0)),
            pl.BlockSpec((N_GATE, qk_w), lambda i, t: (0, 0)),
            pl.BlockSpec((1, qk_w), lambda i, t: (0, 0)),
            pl.BlockSpec((1, GLA_DV), lambda i, t: (0, 0)),
        ],
        out_specs=pl.BlockSpec((None, blk, D_MODEL), lambda i, t: (i, t, 0)),
        out_shape=jax.ShapeDtypeStruct((b, s, D_MODEL), BF16),
        scratch_shapes=[pltpu.VMEM((GLA_HEADS, GLA_DV, GLA_DK), F32)],
        compiler_params=_params("arbitrary", "arbitrary"),
        name="gla",
    )(proj, proj, proj, proj, gates, w_g2p, b_g2.reshape(1, qk_w), norm_gain.reshape(1, GLA_DV))


def _mlstm_kernel(q_ref, k_ref, v_ref, og_ref, g_ref, bias_ref, gain_ref, o_ref,
                  cs_ref, ns_ref, m_ref, *, block):
    @pl.when(pl.program_id(1) == 0)
    def _():
        cs_ref[...] = jnp.zeros_like(cs_ref)
        ns_ref[...] = jnp.zeros_like(ns_ref)
        m_ref[...] = jnp.full_like(m_ref, ML_M_INIT)

    row = lax.broadcasted_iota(jnp.int32, (block, block), 0)
    col = lax.broadcasted_iota(jnp.int32, (block, block), 1)
    causal = col <= row
    tri = causal.astype(BF16)

    capped = ML_GATE_CAP * jnp.tanh((g_ref[...] + bias_ref[...]) * (1.0 / ML_GATE_CAP))
    log_i = capped
    cum = _cumsum_rows(tri, _log_sigmoid(capped))
    log_i_t = log_i.T
    cum_t = cum.T
    gain = gain_ref[...]

    for h in range(ML_HEADS):
        li_c = log_i[:, LANE_ML_I + h:LANE_ML_I + h + 1]
        li_r = log_i_t[LANE_ML_I + h:LANE_ML_I + h + 1, :]
        cum_c = cum[:, LANE_ML_F + h:LANE_ML_F + h + 1]
        cum_r = cum_t[LANE_ML_F + h:LANE_ML_F + h + 1, :]
        m_prev = m_ref[h:h + 1, 0:1]

        q = q_ref[:, h * ML_DK:(h + 1) * ML_DK].astype(F32) * (ML_DK ** -0.5)
        qb = q.astype(BF16)
        kb = k_ref[:, h * ML_DK:(h + 1) * ML_DK]
        v = v_ref[:, h * ML_DV:(h + 1) * ML_DV]

        log_d = jnp.where(causal, cum_c - cum_r + li_r, -jnp.inf)
        m_intra = jnp.max(log_d, axis=1, keepdims=True)
        m_inter = cum_c + m_prev
        m_t = jnp.maximum(m_inter, m_intra)
        a_inter = jnp.exp(m_inter - m_t)
        p = _dot_nt(qb, kb) * jnp.exp(log_d - m_t)
        num = _dot(p.astype(BF16), v) + a_inter * _dot(qb, cs_ref[h].astype(BF16))
        den = (jnp.sum(p, axis=1, keepdims=True)
               + a_inter * jnp.sum(q * ns_ref[h:h + 1, :], axis=1, keepdims=True))
        hid = num / jnp.maximum(jnp.abs(den), jnp.exp(-m_t))

        cum_last = cum_c[block - 1:block, :]
        log_end = cum_last - cum_c + li_c
        m_new = jnp.maximum(cum_last + m_prev, jnp.max(log_end, axis=0, keepdims=True))
        dec = jnp.exp(cum_last + m_prev - m_new)
        kw = kb.astype(F32) * jnp.exp(log_end - m_new)
        cs_ref[h] = dec * cs_ref[h] + _dot_tn(kw.astype(BF16), v)
        ns_ref[h:h + 1, :] = dec * ns_ref[h:h + 1, :] + jnp.sum(kw, axis=0, keepdims=True)
        m_ref[h:h + 1, :] = jnp.broadcast_to(m_new, (1, LANES))

        og = og_ref[:, h * ML_DV:(h + 1) * ML_DV].astype(F32)
        o_ref[:, h * ML_DV:(h + 1) * ML_DV] = (_rms(hid, gain) * _sigmoid(og)).astype(o_ref.dtype)


def _mlstm(proj, gates, i_bias, f_bias, norm_gain):
    b, s, _ = proj.shape
    blk = min(MIX_BLOCK, s)
    qk_w = ML_HEADS * ML_DK
    bias = jnp.zeros((1, N_GATE), F32)
    bias = bias.at[0, LANE_ML_I:LANE_ML_I + ML_HEADS].set(i_bias)
    bias = bias.at[0, LANE_ML_F:LANE_ML_F + ML_HEADS].set(f_bias)
    return pl.pallas_call(
        functools.partial(_mlstm_kernel, block=blk),
        grid=(b, s // blk),
        in_specs=[
            pl.BlockSpec((None, blk, qk_w), lambda i, t: (i, t, OFF_ML_Q // qk_w)),
            pl.BlockSpec((None, blk, qk_w), lambda i, t: (i, t, OFF_ML_K // qk_w)),
            pl.BlockSpec((None, blk, D_MODEL), lambda i, t: (i, t, OFF_ML_V // D_MODEL)),
            pl.BlockSpec((None, blk, D_MODEL), lambda i, t: (i, t, OFF_ML_O // D_MODEL)),
            pl.BlockSpec((None, blk, N_GATE), lambda i, t: (i, t, 0)),
            pl.BlockSpec((1, N_GATE), lambda i, t: (0, 0)),
            pl.BlockSpec((1, ML_DV), lambda i, t: (0, 0)),
        ],
        out_specs=pl.BlockSpec((None, blk, D_MODEL), lambda i, t: (i, t, 0)),
        out_shape=jax.ShapeDtypeStruct((b, s, D_MODEL), BF16),
        scratch_shapes=[
            pltpu.VMEM((ML_HEADS, ML_DK, ML_DV), F32),
            pltpu.VMEM((SUBLANES, ML_DK), F32),
            pltpu.VMEM((SUBLANES, LANES), F32),
        ],
        compiler_params=_params("arbitrary", "arbitrary"),
        name="mlstm",
    )(proj, proj, proj, proj, gates, bias, norm_gain.reshape(1, ML_DV))


def _gdn_kernel(qkv_ref, z_ref, g_ref, conv_ref, prm_ref, gain_ref, o_ref, buf_ref, s_ref, *, block):
    @pl.when(pl.program_id(1) == 0)
    def _():
        buf_ref[0:SUBLANES, :] = jnp.zeros((SUBLANES, GDN_QKV), F32)
        s_ref[...] = jnp.zeros_like(s_ref)

    buf_ref[SUBLANES:SUBLANES + block, :] = qkv_ref[...].astype(F32)

    def conv_silu(c0):
        acc = None
        for j in range(GDN_CONV):
            r0 = SUBLANES - (GDN_CONV - 1) + j
            term = conv_ref[j:j + 1, c0:c0 + LANES] * buf_ref[r0:r0 + block, c0:c0 + LANES]
            acc = term if acc is None else acc + term
        return _silu(acc)

    row = lax.broadcasted_iota(jnp.int32, (block, block), 0)
    col = lax.broadcasted_iota(jnp.int32, (block, block), 1)
    causal = col <= row
    strict = col < row
    tri = causal.astype(BF16)
    eye = (col == row).astype(F32)
    level_masks = [
        ((row >> j == col >> j) & (row >> (j - 1) != col >> (j - 1))).astype(F32)
        for j in range(1, block.bit_length())
    ]

    g = g_ref[...]
    decay =-jnp.exp(prm_ref[0:1, :]) * _softplus(g + prm_ref[1:2, :])
    cum = _cumsum_rows(tri, decay)
    cum_t = cum.T
    beta = _sigmoid(g)
    gain = gain_ref[...]

    for h in range(GDN_HEADS):
        cum_c = cum[:, LANE_GDN_A + h:LANE_GDN_A + h + 1]
        cum_r = cum_t[LANE_GDN_A + h:LANE_GDN_A + h + 1, :]
        beta_c = beta[:, LANE_GDN_B + h:LANE_GDN_B + h + 1]

        q = conv_silu(h * GDN_DK)
        k = conv_silu(GDN_HEADS * GDN_DK + h * GDN_DK)
        v = conv_silu(2 * GDN_HEADS * GDN_DK + h * GDN_DV)
        q = q * lax.rsqrt(jnp.sum(q * q, axis=-1, keepdims=True) + NORM_EPS) * (GDN_DK ** -0.5)
        k = k * lax.rsqrt(jnp.sum(k * k, axis=-1, keepdims=True) + NORM_EPS)
        qb = q.astype(BF16)
        kb = k.astype(BF16)

        gamma = jnp.where(causal, jnp.exp(jnp.minimum(cum_c - cum_r, 0.0)), 0.0)
        a = jnp.where(strict, beta_c * _dot_nt(kb, kb) * gamma, 0.0)
        inv = eye - a * level_masks[0]
        for mask in level_masks[1:]:
            xb = inv.astype(BF16)
            inv = inv - _dot(_dot(xb, (a * mask).astype(BF16)).astype(BF16), xb)

        e_cum = jnp.exp(cum_c)
        rhs = jnp.concatenate([v * beta_c, k * (beta_c * e_cum)], axis=1).astype(BF16)
        uw = _dot(inv.astype(BF16), rhs)
        u = uw[:, :GDN_DV]
        w = uw[:, GDN_DV:]

        st = s_ref[h]
        sb = st.astype(BF16)
        v_new = u - _dot(w.astype(BF16), sb)
        vb = v_new.astype(BF16)
        a_qk = _dot_nt(qb, kb) * gamma
        o = _dot((q * e_cum).astype(BF16), sb) + _dot(a_qk.astype(BF16), vb)
        cum_last = cum_c[block - 1:block, :]
        k_end = (k * jnp.exp(cum_last - cum_c)).astype(BF16)
        s_ref[h] = st * jnp.exp(cum_last) + _dot_tn(k_end, vb)

        z = z_ref[:, h * GDN_DV:(h + 1) * GDN_DV].astype(F32)
        o_ref[:, h * GDN_DV:(h + 1) * GDN_DV] = (_rms(o, gain) * _silu(z)).astype(o_ref.dtype)

    buf_ref[0:SUBLANES, :] = buf_ref[block:block + SUBLANES, :]


def _gdn(proj, gates, conv_w, a_log, dt_bias, norm_gain):
    b, s, _ = proj.shape
    blk = min(MIX_BLOCK, s)
    prm = jnp.zeros((SUBLANES, N_GATE), F32)
    prm = prm.at[0, LANE_GDN_A:LANE_GDN_A + GDN_HEADS].set(a_log)
    prm = prm.at[1, LANE_GDN_A:LANE_GDN_A + GDN_HEADS].set(dt_bias)
    return pl.pallas_call(
        functools.partial(_gdn_kernel, block=blk),
        grid=(b, s // blk),
        in_specs=[
            pl.BlockSpec((None, blk, GDN_QKV), lambda i, t: (i, t, OFF_GDN_QKV // GDN_QKV)),
            pl.BlockSpec((None, blk, D_MODEL), lambda i, t: (i, t, OFF_GDN_Z // D_MODEL)),
            pl.BlockSpec((None, blk, N_GATE), lambda i, t: (i, t, 0)),
            pl.BlockSpec((GDN_CONV, GDN_QKV), lambda i, t: (0, 0)),
            pl.BlockSpec((SUBLANES, N_GATE), lambda i, t: (0, 0)),
            pl.BlockSpec((1, GDN_DV), lambda i, t: (0, 0)),
        ],
        out_specs=pl.BlockSpec((None, blk, D_MODEL), lambda i, t: (i, t, 0)),
        out_shape=jax.ShapeDtypeStruct((b, s, D_MODEL), BF16),
        scratch_shapes=[
            pltpu.VMEM((blk + SUBLANES, GDN_QKV), F32),
            pltpu.VMEM((GDN_HEADS, GDN_DK, GDN_DV), F32),
        ],
        compiler_params=_params("arbitrary", "arbitrary"),
        name="gdn",
    )(proj, proj, gates, conv_w, prm, norm_gain.reshape(1, GDN_DV))


def _merge_kernel(yg_ref, ym_ref, yd_ref, m0_ref, m1_ref, m2_ref, x_ref, mod_ref, w_ref, o_ref):
    y = (_sigmoid(m0_ref[...].astype(F32)) * yg_ref[...].astype(F32)
         + _sigmoid(m1_ref[...].astype(F32)) * ym_ref[...].astype(F32)
         + _sigmoid(m2_ref[...].astype(F32)) * yd_ref[...].astype(F32))
    o_ref[...] = x_ref[...] + mod_ref[2:3, :] * _dot(y.astype(BF16), w_ref[...])


def _merge(y_gla, y_ml, y_gdn, proj2, x2, mod, w_out, seq):
    t, d = x2.shape
    tm = min(TOK_TM, seq)
    per_seq = seq // tm
    tok = pl.BlockSpec((tm, d), lambda i: (i, 0))

    def merge_spec(n):
        return pl.BlockSpec((tm, d), lambda i: (i, OFF_MERGE // d + n))

    return pl.pallas_call(
        _merge_kernel,
        grid=(t // tm,),
        in_specs=[tok, tok, tok, merge_spec(0), merge_spec(1), merge_spec(2), tok,
                  pl.BlockSpec((None, 6, d), lambda i: (i // per_seq, 0, 0)),
                  pl.BlockSpec((d, d), lambda i: (0, 0))],
        out_specs=tok,
        out_shape=jax.ShapeDtypeStruct((t, d), F32),
        compiler_params=_params("arbitrary"),
        name="merge",
    )(y_gla, y_ml, y_gdn, proj2, proj2, proj2, x2, mod, w_out)


def _ffn_kernel(x_ref, mod_ref, gain_ref, wu_ref, wd_ref, fin_ref, o_ref, *, final_norm):
    x = x_ref[...]
    h = (_rms(x, gain_ref[...]) * (1.0 + mod_ref[4:5, :]) + mod_ref[3:4, :]).astype(BF16)
    acc = None
    for c in range(D_FF // FF_CHUNK):
        up = _dot(h, wu_ref[:, 2 * c * FF_CHUNK:2 * (c + 1) * FF_CHUNK])
        act = (_silu(up[:, :FF_CHUNK]) * up[:, FF_CHUNK:]).astype(BF16)
        part = _dot(act, wd_ref[c * FF_CHUNK:(c + 1) * FF_CHUNK, :])
        acc = part if acc is None else acc + part
    y = x + mod_ref[5:6, :] * acc
    if final_norm:
        y = _rms(y, fin_ref[...])
    o_ref[...] = y


def _ffn(x2, mod, gain, w_up, w_down, norm_final, seq, final_norm):
    t, d = x2.shape
    tm = min(TOK_TM, seq)
    per_seq = seq // tm
    tok = pl.BlockSpec((tm, d), lambda i: (i, 0))
    return pl.pallas_call(
        functools.partial(_ffn_kernel, final_norm=final_norm),
        grid=(t // tm,),
        in_specs=[tok,
                  pl.BlockSpec((None, 6, d), lambda i: (i // per_seq, 0, 0)),
                  pl.BlockSpec((1, d), lambda i: (0, 0)),
                  pl.BlockSpec((d, 2 * D_FF), lambda i: (0, 0)),
                  pl.BlockSpec((D_FF, d), lambda i: (0, 0)),
                  pl.BlockSpec((1, d), lambda i: (0, 0))],
        out_specs=tok,
        out_shape=jax.ShapeDtypeStruct((t, d), F32),
        compiler_params=_params("arbitrary"),
        name="ffn",
    )(x2, mod, gain, w_up, w_down, norm_final)


def _layout_w_in(w):
    sizes = (512, 512, 1024, 1024, GLA_GATE_RANK, 512, 512, 1024, 1024, ML_HEADS, ML_HEADS,
             GDN_QKV, 1024, GDN_HEADS, GDN_HEADS, 3 * D_MODEL)
    parts, o = [], 0
    for n in sizes:
        parts.append(w[:, o:o + n])
        o += n
    (gla_q, gla_k, gla_v, gla_r, gla_glr, ml_q, ml_k, ml_v, ml_o, ml_i, ml_f,
     gdn_qkv, gdn_z, gdn_a, gdn_b, merge) = parts
    main = jnp.concatenate([gdn_qkv, merge, gla_v, gla_r, ml_v, ml_o, gdn_z,
                            gla_q, gla_k, ml_q, ml_k], axis=1).astype(BF16)
    small = jnp.concatenate([gla_glr, ml_i, ml_f, gdn_a, gdn_b], axis=1)
    gate = jnp.pad(small, ((0, 0), (0, N_GATE - small.shape[1]))).astype(BF16)
    return main, gate


def _layout_w_up(w):
    d = w.shape[0]
    n = D_FF // FF_CHUNK
    g = w[:, :D_FF].reshape(d, n, FF_CHUNK)
    v = w[:, D_FF:].reshape(d, n, FF_CHUNK)
    return jnp.stack([g, v], axis=2).reshape(d, 2 * D_FF).astype(BF16)


def kernel(x, c, ada_w, ada_b, norm_mix, norm_ffn, w_in, gla_w_g2, gla_b_g2, gla_norm, ml_i_bias, ml_f_bias, ml_norm, gdn_conv, gdn_a_log, gdn_dt_bias, gdn_norm, w_out, w_ffn_up, w_ffn_down, norm_final):
    b, s, d = x.shape
    depth = ada_w.shape[0]
    mod = _adaln(c, ada_w, ada_b).reshape(depth, b, 6, d)
    x2 = x.reshape(b * s, d)
    fin = norm_final.reshape(1, d)
    for l in range(depth):
        w_main, w_gate = _layout_w_in(w_in[l])
        proj, gates = _inproj(x2, mod[l], norm_mix[l].reshape(1, d), w_main, w_gate, s)
        proj3 = proj.reshape(b, s, N_MAIN)
        gates3 = gates.reshape(b, s, N_GATE)
        y_gla = _gla(proj3, gates3, gla_w_g2[l], gla_b_g2[l], gla_norm[l])
        y_ml = _mlstm(proj3, gates3, ml_i_bias[l], ml_f_bias[l], ml_norm[l])
        y_gdn = _gdn(proj3, gates3, gdn_conv[l], gdn_a_log[l], gdn_dt_bias[l], gdn_norm[l])
        x2 = _merge(y_gla.reshape(b * s, d), y_ml.reshape(b * s, d), y_gdn.reshape(b * s, d),
                    proj, x2, mod[l], w_out[l].astype(BF16), s)
        x2 = _ffn(x2, mod[l], norm_ffn[l].reshape(1, d), _layout_w_up(w_ffn_up[l]),
                  w_ffn_down[l].astype(BF16), fin, s, final_norm=(l == depth - 1))
    return x2.reshape(b, s, d)
```

```python
import functools

import jax
import jax.numpy as jnp
from jax import lax
from jax.experimental import pallas as pl
from jax.experimental.pallas import tpu as pltpu

F32 = jnp.float32
BF16 = jnp.bfloat16

D_MODEL = 1024
NORM_EPS = 1e-6
GLA_HEADS, GLA_DK, GLA_DV = 4, 128, 256
GLA_GATE_RANK = 16
GLA_GATE_NORM = 16.0
GLA_SUB = 16
ML_HEADS, ML_DK, ML_DV = 4, 128, 256
ML_GATE_CAP = 15.0
ML_M_INIT = -1e30
GDN_HEADS, GDN_DK, GDN_DV = 8, 128, 128
GDN_CONV = 4
GDN_QKV = GDN_HEADS * (2 * GDN_DK + GDN_DV)
D_FF = 2816
FF_CHUNK = 256

OFF_GDN_QKV = 0
OFF_MERGE = 3072
OFF_GLA_V = 6144
OFF_GLA_R = 7168
OFF_ML_V = 8192
OFF_ML_O = 9216
OFF_GDN_Z = 10240
OFF_GLA_Q = 11264
OFF_GLA_K = 11776
OFF_ML_Q = 12288
OFF_ML_K = 12800
N_MAIN = 13312
N_GATE = 128
LANE_GLR = 0
LANE_ML_I = 16
LANE_ML_F = 20
LANE_GDN_A = 24
LANE_GDN_B = 32

LANES = 128
SUBLANES = 8
VMEM_LIMIT = 56 * 1024 * 1024

MIX_BLOCK = 128
PROJ_TM = 1024
PROJ_TN = 1024
TOK_TM = 512


def _dot(a, b):
    return jnp.dot(a, b, preferred_element_type=F32)


def _dot_nt(a, b):
    return lax.dot_general(a, b, (((1,), (1,)), ((), ())), preferred_element_type=F32)


def _dot_tn(a, b):
    return lax.dot_general(a, b, (((0,), (0,)), ((), ())), preferred_element_type=F32)


def _split3(x):
    hi = x.astype(BF16)
    r = x - hi.astype(F32)
    mid = r.astype(BF16)
    lo = (r - mid.astype(F32)).astype(BF16)
    return hi, mid, lo


def _dot_f32(a, b):
    a0, a1, a2 = _split3(a)
    b0, b1, b2 = _split3(b)
    small = _dot(a0, b2) + _dot(a2, b0) + _dot(a1, b1)
    return _dot(a0, b0) + (_dot(a0, b1) + _dot(a1, b0) + small)


def _cumsum_rows(tri, x):
    hi, mid, lo = _split3(x)
    return _dot(tri, hi) + (_dot(tri, mid) + _dot(tri, lo))


def _log_sigmoid(x):
    return jnp.minimum(x, 0.0) - jnp.log1p(jnp.exp(-jnp.abs(x)))


def _softplus(x):
    return jnp.maximum(x, 0.0) + jnp.log1p(jnp.exp(-jnp.abs(x)))


def _sigmoid(x):
    return 1.0 / (1.0 + jnp.exp(-x))


def _silu(x):
    return x * _sigmoid(x)


def _rms(x, gain):
    return x * lax.rsqrt(jnp.mean(x * x, axis=-1, keepdims=True) + NORM_EPS) * gain


def _params(*sem):
    return pltpu.CompilerParams(dimension_semantics=sem, vmem_limit_bytes=VMEM_LIMIT)


def _adaln_kernel(c_ref, w_ref, b_ref, o_ref):
    c = c_ref[...]
    o_ref[...] = _dot_f32(_silu(c), w_ref[...]) + b_ref[...]


def _adaln(c, ada_w, ada_b):
    depth, d, n = ada_w.shape
    b = c.shape[0]
    tn = D_MODEL
    return pl.pallas_call(
        _adaln_kernel,
        grid=(depth, n // tn),
        in_specs=[
            pl.BlockSpec((b, d), lambda l, j: (0, 0)),
            pl.BlockSpec((None, d, tn), lambda l, j: (l, 0, j)),
            pl.BlockSpec((None, 1, tn), lambda l, j: (l, 0, j)),
        ],
        out_specs=pl.BlockSpec((None, b, tn), lambda l, j: (l, 0, j)),
        out_shape=jax.ShapeDtypeStruct((depth, b, n), F32),
        compiler_params=_params("arbitrary", "arbitrary"),
        name="adaln",
    )(c, ada_w, ada_b.reshape(depth, 1, n))


def _inproj_kernel(x_ref, mod_ref, gain_ref, w_ref, wg_ref, o_ref, og_ref, h_ref):
    @pl.when(pl.program_id(1) == 0)
    def _():
        h = _rms(x_ref[...], gain_ref[...]) * (1.0 + mod_ref[1:2, :]) + mod_ref[0:1, :]
        hb = h.astype(BF16)
        h_ref[...] = hb
        og_ref[...] = _dot(hb, wg_ref[...])

    o_ref[...] = _dot(h_ref[...], w_ref[...]).astype(BF16)


def _inproj(x2, mod, gain, w_main, w_gate, seq):
    t, d = x2.shape
    tm = min(PROJ_TM, seq)
    per_seq = seq // tm
    return pl.pallas_call(
        _inproj_kernel,
        grid=(t // tm, N_MAIN // PROJ_TN),
        in_specs=[
            pl.BlockSpec((tm, d), lambda i, j: (i, 0)),
            pl.BlockSpec((None, 6, d), lambda i, j: (i // per_seq, 0, 0)),
            pl.BlockSpec((1, d), lambda i, j: (0, 0)),
            pl.BlockSpec((d, PROJ_TN), lambda i, j: (0, j)),
            pl.BlockSpec((d, N_GATE), lambda i, j: (0, 0)),
        ],
        out_specs=[
            pl.BlockSpec((tm, PROJ_TN), lambda i, j: (i, j)),
            pl.BlockSpec((tm, N_GATE), lambda i, j: (i, 0)),
        ],
        out_shape=[
            jax.ShapeDtypeStruct((t, N_MAIN), BF16),
            jax.ShapeDtypeStruct((t, N_GATE), F32),
        ],
        scratch_shapes=[pltpu.VMEM((tm, d), BF16)],
        compiler_params=_params("arbitrary", "arbitrary"),
        name="inproj",
    )(x2, mod, gain, w_main, w_gate)


def _row_refs(c, block, seg, pick):
    parts = []
    for start in range(0, block, seg):
        r = pick(start)
        row = jnp.zeros((1, c.shape[1]), F32) if r is None else c[r:r + 1, :]
        parts.append(jnp.broadcast_to(row, (seg, c.shape[1])))
    return parts[0] if len(parts) == 1 else jnp.concatenate(parts, axis=0)


def _gla_kernel(q_ref, k_ref, v_ref, r_ref, g_ref, wg_ref, bg_ref, gain_ref, o_ref, st_ref, *, block):
    @pl.when(pl.program_id(1) == 0)
    def _():
        st_ref[...] = jnp.zeros_like(st_ref)

    row = lax.broadcasted_iota(jnp.int32, (block, block), 0)
    col = lax.broadcasted_iota(jnp.int32, (block, block), 1)
    tri = (col <= row).astype(BF16)
    sub_xor = (row // GLA_SUB) ^ (col // GLA_SUB)
    causal = col <= row

    g = g_ref[...]
    g0, g1, _ = _split3(g)
    w0, w1, _ = _split3(wg_ref[...])
    x = _dot(g0, w0) + (_dot(g0, w1) + _dot(g1, w0)) + bg_ref[...]
    log_alpha = _log_sigmoid(x) * (1.0 / GLA_GATE_NORM)
    cum = _cumsum_rows(tri, log_alpha)

    n_levels = (block // GLA_SUB).bit_length() - 1
    gain = gain_ref[...]
    heads = range(GLA_HEADS)
    c = [cum[:, h * GLA_DK:(h + 1) * GLA_DK] for h in heads]
    q = [q_ref[:, h * GLA_DK:(h + 1) * GLA_DK].astype(F32) * (GLA_DK ** -0.5) for h in heads]
    k = [k_ref[:, h * GLA_DK:(h + 1) * GLA_DK].astype(F32) for h in heads]
    v = [v_ref[:, h * GLA_DV:(h + 1) * GLA_DV] for h in heads]

    ref0 = [_row_refs(c[h], block, GLA_SUB, lambda s: None if s == 0 else s - 1) for h in heads]
    p = [_dot_nt((q[h] * jnp.exp(c[h] - ref0[h])).astype(BF16),
                 (k[h] * jnp.exp(ref0[h] - c[h])).astype(BF16)) for h in heads]
    a = [jnp.where(causal & (sub_xor == 0), p[h], 0.0) for h in heads]
    for lev in range(1, n_levels + 1):
        half = GLA_SUB << (lev - 1)
        sel = causal & (sub_xor >= (1 << (lev - 1))) & (sub_xor < (1 << lev))
        ref = [_row_refs(c[h], block, 2 * half, lambda s: s + half - 1) for h in heads]
        qs = [(q[h] * jnp.exp(jnp.minimum(c[h] - ref[h], 0.0))).astype(BF16) for h in heads]
        ks = [(k[h] * jnp.exp(jnp.minimum(ref[h] - c[h], 0.0))).astype(BF16) for h in heads]
        p = [_dot_nt(qs[h], ks[h]) for h in heads]
        a = [jnp.where(sel, p[h], a[h]) for h in heads]

    st = [st_ref[h] for h in heads]
    o_intra = [_dot(a[h].astype(BF16), v[h]) for h in heads]
    o_inter = [_dot_nt((q[h] * jnp.exp(c[h])).astype(BF16), st[h].astype(BF16)) for h in heads]
    for h in heads:
        c_end = c[h][block - 1:block, :]
        k_end = (k[h] * jnp.exp(c_end - c[h])).astype(BF16)
        st_ref[h] = st[h] * jnp.exp(c_end) + _dot_tn(v[h], k_end)
    for h in heads:
        r = r_ref[:, h * GLA_DV:(h + 1) * GLA_DV].astype(F32)
        o = o_intra[h] + o_inter[h]
        o_ref[:, h * GLA_DV:(h + 1) * GLA_DV] = (_rms(o, gain) * _silu(r)).astype(o_ref.dtype)


def _gla(proj, gates, w_g2, b_g2, norm_gain):
    b, s, _ = proj.shape
    blk = min(MIX_BLOCK, s)
    qk_w = GLA_HEADS * GLA_DK
    w_g2p = jnp.zeros((N_GATE, qk_w), F32).at[LANE_GLR:LANE_GLR + GLA_GATE_RANK].set(w_g2)
    return pl.pallas_call(
        functools.partial(_gla_kernel, block=blk),
        grid=(b, s // blk),
        in_specs=[
            pl.BlockSpec((None, blk, qk_w), lambda i, t: (i, t, OFF_GLA_Q // qk_w)),
            pl.BlockSpec((None, blk, qk_w), lambda i, t: (i, t, OFF_GLA_K // qk_w)),
            pl.BlockSpec((None, blk, D_MODEL), lambda i, t: (i, t, OFF_GLA_V // D_MODEL)),
            pl.BlockSpec((None, blk, D_MODEL), lambda i, t: (i, t, OFF_GLA_R // D_MODEL)),
            pl.BlockSpec((None, blk, N_GATE), lambda i, t: (i, t, 0)),
            pl.BlockSpec((N_GATE, qk_w), lambda i, t: (0, 0)),
            pl.BlockSpec((1, qk_w), lambda i, t: (0, 0)),
            pl.BlockSpec((1, GLA_DV), lambda i, t: (0, 0)),
        ],
        out_specs=pl.BlockSpec((None, blk, D_MODEL), lambda i, t: (i, t, 0)),
        out_shape=jax.ShapeDtypeStruct((b, s, D_MODEL), BF16),
        scratch_shapes=[pltpu.VMEM((GLA_HEADS, GLA_DV, GLA_DK), F32)],
        compiler_params=_params("arbitrary", "arbitrary"),
        name="gla",
    )(proj, proj, proj, proj, gates, w_g2p, b_g2.reshape(1, qk_w), norm_gain.reshape(1, GLA_DV))


def _mlstm_kernel(q_ref, k_ref, v_ref, og_ref, g_ref, bias_ref, gain_ref, o_ref,
                  cs_ref, ns_ref, m_ref, *, block):
    @pl.when(pl.program_id(1) == 0)
    def _():
        cs_ref[...] = jnp.zeros_like(cs_ref)
        ns_ref[...] = jnp.zeros_like(ns_ref)
        m_ref[...] = jnp.full_like(m_ref, ML_M_INIT)

    row = lax.broadcasted_iota(jnp.int32, (block, block), 0)
    col = lax.broadcasted_iota(jnp.int32, (block, block), 1)
    causal = col <= row
    tri = causal.astype(BF16)

    capped = ML_GATE_CAP * jnp.tanh((g_ref[...] + bias_ref[...]) * (1.0 / ML_GATE_CAP))
    log_i = capped
    cum = _cumsum_rows(tri, _log_sigmoid(capped))
    log_i_t = log_i.T
    cum_t = cum.T
    gain = gain_ref[...]

    heads = range(ML_HEADS)
    li_c = [log_i[:, LANE_ML_I + h:LANE_ML_I + h + 1] for h in heads]
    li_r = [log_i_t[LANE_ML_I + h:LANE_ML_I + h + 1, :] for h in heads]
    cum_c = [cum[:, LANE_ML_F + h:LANE_ML_F + h + 1] for h in heads]
    cum_r = [cum_t[LANE_ML_F + h:LANE_ML_F + h + 1, :] for h in heads]
    m_prev = [m_ref[h:h + 1, 0:1] for h in heads]
    q = [q_ref[:, h * ML_DK:(h + 1) * ML_DK].astype(F32) * (ML_DK ** -0.5) for h in heads]
    qb = [t.astype(BF16) for t in q]
    kb = [k_ref[:, h * ML_DK:(h + 1) * ML_DK] for h in heads]
    v = [v_ref[:, h * ML_DV:(h + 1) * ML_DV] for h in heads]
    cs = [cs_ref[h] for h in heads]
    ns = [ns_ref[h:h + 1, :] for h in heads]

    qk = [_dot_nt(qb[h], kb[h]) for h in heads]
    q_cs = [_dot(qb[h], cs[h].astype(BF16)) for h in heads]
    log_d = [jnp.where(causal, cum_c[h] - cum_r[h] + li_r[h], -jnp.inf) for h in heads]
    m_inter = [cum_c[h] + m_prev[h] for h in heads]
    m_t = [jnp.maximum(m_inter[h], jnp.max(log_d[h], axis=1, keepdims=True)) for h in heads]
    a_inter = [jnp.exp(m_inter[h] - m_t[h]) for h in heads]
    p = [qk[h] * jnp.exp(log_d[h] - m_t[h]) for h in heads]
    pv = [_dot(p[h].astype(BF16), v[h]) for h in heads]

    cum_last = [cum_c[h][block - 1:block, :] for h in heads]
    log_end = [cum_last[h] - cum_c[h] + li_c[h] for h in heads]
    m_new = [jnp.maximum(cum_last[h] + m_prev[h], jnp.max(log_end[h], axis=0, keepdims=True))
             for h in heads]
    dec = [jnp.exp(cum_last[h] + m_prev[h] - m_new[h]) for h in heads]
    kw = [kb[h].astype(F32) * jnp.exp(log_end[h] - m_new[h]) for h in heads]
    kwv = [_dot_tn(kw[h].astype(BF16), v[h]) for h in heads]
    for h in heads:
        cs_ref[h] = dec[h] * cs[h] + kwv[h]
        ns_ref[h:h + 1, :] = dec[h] * ns[h] + jnp.sum(kw[h], axis=0, keepdims=True)
        m_ref[h:h + 1, :] = jnp.broadcast_to(m_new[h], (1, LANES))
    for h in heads:
        num = pv[h] + a_inter[h] * q_cs[h]
        den = (jnp.sum(p[h], axis=1, keepdims=True)
               + a_inter[h] * jnp.sum(q[h] * ns[h], axis=1, keepdims=True))
        hid = num / jnp.maximum(jnp.abs(den), jnp.exp(-m_t[h]))
        og = og_ref[:, h * ML_DV:(h + 1) * ML_DV].astype(F32)
        o_ref[:, h * ML_DV:(h + 1) * ML_DV] = (_rms(hid, gain) * _sigmoid(og)).astype(o_ref.dtype)


def _mlstm(proj, gates, i_bias, f_bias, norm_gain):
    b, s, _ = proj.shape
    blk = min(MIX_BLOCK, s)
    qk_w = ML_HEADS * ML_DK
    bias = jnp.zeros((1, N_GATE), F32)
    bias = bias.at[0, LANE_ML_I:LANE_ML_I + ML_HEADS].set(i_bias)
    bias = bias.at[0, LANE_ML_F:LANE_ML_F + ML_HEADS].set(f_bias)
    return pl.pallas_call(
        functools.partial(_mlstm_kernel, block=blk),
        grid=(b, s // blk),
        in_specs=[
            pl.BlockSpec((None, blk, qk_w), lambda i, t: (i, t, OFF_ML_Q // qk_w)),
            pl.BlockSpec((None, blk, qk_w), lambda i, t: (i, t, OFF_ML_K // qk_w)),
            pl.BlockSpec((None, blk, D_MODEL), lambda i, t: (i, t, OFF_ML_V // D_MODEL)),
            pl.BlockSpec((None, blk, D_MODEL), lambda i, t: (i, t, OFF_ML_O // D_MODEL)),
            pl.BlockSpec((None, blk, N_GATE), lambda i, t: (i, t, 0)),
            pl.BlockSpec((1, N_GATE), lambda i, t: (0, 0)),
            pl.BlockSpec((1, ML_DV), lambda i, t: (0, 0)),
        ],
        out_specs=pl.BlockSpec((None, blk, D_MODEL), lambda i, t: (i, t, 0)),
        out_shape=jax.ShapeDtypeStruct((b, s, D_MODEL), BF16),
        scratch_shapes=[
            pltpu.VMEM((ML_HEADS, ML_DK, ML_DV), F32),
            pltpu.VMEM((SUBLANES, ML_DK), F32),
            pltpu.VMEM((SUBLANES, LANES), F32),
        ],
        compiler_params=_params("arbitrary", "arbitrary"),
        name="mlstm",
    )(proj, proj, proj, proj, gates, bias, norm_gain.reshape(1, ML_DV))


def _gdn_kernel(qkv_ref, z_ref, g_ref, conv_ref, prm_ref, gain_ref, o_ref, buf_ref, s_ref, *, block):
    @pl.when(pl.program_id(1) == 0)
    def _():
        buf_ref[0:SUBLANES, :] = jnp.zeros((SUBLANES, GDN_QKV), F32)
        s_ref[...] = jnp.zeros_like(s_ref)

    buf_ref[SUBLANES:SUBLANES + block, :] = qkv_ref[...].astype(F32)

    def conv_silu(c0):
        acc = None
        for j in range(GDN_CONV):
            r0 = SUBLANES - (GDN_CONV - 1) + j
            term = conv_ref[j:j + 1, c0:c0 + LANES] * buf_ref[r0:r0 + block, c0:c0 + LANES]
            acc = term if acc is None else acc + term
        return _silu(acc)

    row = lax.broadcasted_iota(jnp.int32, (block, block), 0)
    col = lax.broadcasted_iota(jnp.int32, (block, block), 1)
    causal = col <= row
    strict = col < row
    tri = causal.astype(BF16)
    eye = (col == row).astype(F32)
    level_masks = [
        ((row >> j == col >> j) & (row >> (j - 1) != col >> (j - 1))).astype(F32)
        for j in range(1, block.bit_length())
    ]

    g = g_ref[...]
    decay =-jnp.exp(prm_ref[0:1, :]) * _softplus(g + prm_ref[1:2, :])
    cum = _cumsum_rows(tri, decay)
    cum_t = cum.T
    beta = _sigmoid(g)
    gain = gain_ref[...]

    heads = range(GDN_HEADS)
    cum_c = [cum[:, LANE_GDN_A + h:LANE_GDN_A + h + 1] for h in heads]
    cum_r = [cum_t[LANE_GDN_A + h:LANE_GDN_A + h + 1, :] for h in heads]
    beta_c = [beta[:, LANE_GDN_B + h:LANE_GDN_B + h + 1] for h in heads]
    q = [conv_silu(h * GDN_DK) for h in heads]
    k = [conv_silu(GDN_HEADS * GDN_DK + h * GDN_DK) for h in heads]
    v = [conv_silu(2 * GDN_HEADS * GDN_DK + h * GDN_DV) for h in heads]
    q = [t * lax.rsqrt(jnp.sum(t * t, axis=-1, keepdims=True) + NORM_EPS) * (GDN_DK ** -0.5) for t in q]
    k = [t * lax.rsqrt(jnp.sum(t * t, axis=-1, keepdims=True) + NORM_EPS) for t in k]
    qb = [t.astype(BF16) for t in q]
    kb = [t.astype(BF16) for t in k]
    gamma = [jnp.where(causal, jnp.exp(jnp.minimum(cum_c[h] - cum_r[h], 0.0)), 0.0) for h in heads]
    kk = [_dot_nt(kb[h], kb[h]) for h in heads]
    a = [jnp.where(strict, beta_c[h] * kk[h] * gamma[h], 0.0) for h in heads]
    inv = [eye - a[h] * level_masks[0] for h in heads]
    for mask in level_masks[1:]:
        xb = [t.astype(BF16) for t in inv]
        xa = [_dot(xb[h], (a[h] * mask).astype(BF16)) for h in heads]
        xax = [_dot(xa[h].astype(BF16), xb[h]) for h in heads]
        inv = [inv[h] - xax[h] for h in heads]

    e_cum = [jnp.exp(t) for t in cum_c]
    rhs = [jnp.concatenate([v[h] * beta_c[h], k[h] * (beta_c[h] * e_cum[h])], axis=1).astype(BF16)
           for h in heads]
    uw = [_dot(inv[h].astype(BF16), rhs[h]) for h in heads]
    st = [s_ref[h] for h in heads]
    sb = [t.astype(BF16) for t in st]
    ws = [_dot(uw[h][:, GDN_DV:].astype(BF16), sb[h]) for h in heads]
    qs = [_dot((q[h] * e_cum[h]).astype(BF16), sb[h]) for h in heads]
    qk = [_dot_nt(qb[h], kb[h]) for h in heads]
    vb = [(uw[h][:, :GDN_DV] - ws[h]).astype(BF16) for h in heads]
    o = [qs[h] + _dot((qk[h] * gamma[h]).astype(BF16), vb[h]) for h in heads]
    for h in heads:
        cum_last = cum_c[h][block - 1:block, :]
        k_end = (k[h] * jnp.exp(cum_last - cum_c[h])).astype(BF16)
        s_ref[h] = st[h] * jnp.exp(cum_last) + _dot_tn(k_end, vb[h])
    for h in heads:
        z = z_ref[:, h * GDN_DV:(h + 1) * GDN_DV].astype(F32)
        o_ref[:, h * GDN_DV:(h + 1) * GDN_DV] = (_rms(o[h], gain) * _silu(z)).astype(o_ref.dtype)

    buf_ref[0:SUBLANES, :] = buf_ref[block:block + SUBLANES, :]


def _gdn(proj, gates, conv_w, a_log, dt_bias, norm_gain):
    b, s, _ = proj.shape
    blk = min(MIX_BLOCK, s)
    prm = jnp.zeros((SUBLANES, N_GATE), F32)
    prm = prm.at[0, LANE_GDN_A:LANE_GDN_A + GDN_HEADS].set(a_log)
    prm = prm.at[1, LANE_GDN_A:LANE_GDN_A + GDN_HEADS].set(dt_bias)
    return pl.pallas_call(
        functools.partial(_gdn_kernel, block=blk),
        grid=(b, s // blk),
        in_specs=[
            pl.BlockSpec((None, blk, GDN_QKV), lambda i, t: (i, t, OFF_GDN_QKV // GDN_QKV)),
            pl.BlockSpec((None, blk, D_MODEL), lambda i, t: (i, t, OFF_GDN_Z // D_MODEL)),
            pl.BlockSpec((None, blk, N_GATE), lambda i, t: (i, t, 0)),
            pl.BlockSpec((GDN_CONV, GDN_QKV), lambda i, t: (0, 0)),
            pl.BlockSpec((SUBLANES, N_GATE), lambda i, t: (0, 0)),
            pl.BlockSpec((1, GDN_DV), lambda i, t: (0, 0)),
        ],
        out_specs=pl.BlockSpec((None, blk, D_MODEL), lambda i, t: (i, t, 0)),
        out_shape=jax.ShapeDtypeStruct((b, s, D_MODEL), BF16),
        scratch_shapes=[
            pltpu.VMEM((blk + SUBLANES, GDN_QKV), F32),
            pltpu.VMEM((GDN_HEADS, GDN_DK, GDN_DV), F32),
        ],
        compiler_params=_params("arbitrary", "arbitrary"),
        name="gdn",
    )(proj, proj, gates, conv_w, prm, norm_gain.reshape(1, GDN_DV))


def _merge_kernel(yg_ref, ym_ref, yd_ref, m0_ref, m1_ref, m2_ref, x_ref, mod_ref, w_ref, o_ref):
    y = (_sigmoid(m0_ref[...].astype(F32)) * yg_ref[...].astype(F32)
         + _sigmoid(m1_ref[...].astype(F32)) * ym_ref[...].astype(F32)
         + _sigmoid(m2_ref[...].astype(F32)) * yd_ref[...].astype(F32))
    o_ref[...] = x_ref[...] + mod_ref[2:3, :] * _dot(y.astype(BF16), w_ref[...])


def _merge(y_gla, y_ml, y_gdn, proj2, x2, mod, w_out, seq):
    t, d = x2.shape
    tm = min(TOK_TM, seq)
    per_seq = seq // tm
    tok = pl.BlockSpec((tm, d), lambda i: (i, 0))

    def merge_spec(n):
        return pl.BlockSpec((tm, d), lambda i: (i, OFF_MERGE // d + n))

    return pl.pallas_call(
        _merge_kernel,
        grid=(t // tm,),
        in_specs=[tok, tok, tok, merge_spec(0), merge_spec(1), merge_spec(2), tok,
                  pl.BlockSpec((None, 6, d), lambda i: (i // per_seq, 0, 0)),
                  pl.BlockSpec((d, d), lambda i: (0, 0))],
        out_specs=tok,
        out_shape=jax.ShapeDtypeStruct((t, d), F32),
        compiler_params=_params("arbitrary"),
        name="merge",
    )(y_gla, y_ml, y_gdn, proj2, proj2, proj2, x2, mod, w_out)


def _ffn_kernel(x_ref, mod_ref, gain_ref, wu_ref, wd_ref, fin_ref, o_ref, *, final_norm):
    x = x_ref[...]
    h = (_rms(x, gain_ref[...]) * (1.0 + mod_ref[4:5, :]) + mod_ref[3:4, :]).astype(BF16)
    acc = None
    for c in range(D_FF // FF_CHUNK):
        up = _dot(h, wu_ref[:, 2 * c * FF_CHUNK:2 * (c + 1) * FF_CHUNK])
        act = (_silu(up[:, :FF_CHUNK]) * up[:, FF_CHUNK:]).astype(BF16)
        part = _dot(act, wd_ref[c * FF_CHUNK:(c + 1) * FF_CHUNK, :])
        acc = part if acc is None else acc + part
    y = x + mod_ref[5:6, :] * acc
    if final_norm:
        y = _rms(y, fin_ref[...])
    o_ref[...] = y


def _ffn(x2, mod, gain, w_up, w_down, norm_final, seq, final_norm):
    t, d = x2.shape
    tm = min(TOK_TM, seq)
    per_seq = seq // tm
    tok = pl.BlockSpec((tm, d), lambda i: (i, 0))
    return pl.pallas_call(
        functools.partial(_ffn_kernel, final_norm=final_norm),
        grid=(t // tm,),
        in_specs=[tok,
                  pl.BlockSpec((None, 6, d), lambda i: (i // per_seq, 0, 0)),
                  pl.BlockSpec((1, d), lambda i: (0, 0)),
                  pl.BlockSpec((d, 2 * D_FF), lambda i: (0, 0)),
                  pl.BlockSpec((D_FF, d), lambda i: (0, 0)),
                  pl.BlockSpec((1, d), lambda i: (0, 0))],
        out_specs=tok,
        out_shape=jax.ShapeDtypeStruct((t, d), F32),
        compiler_params=_params("arbitrary"),
        name="ffn",
    )(x2, mod, gain, w_up, w_down, norm_final)


def _layout_w_in(w):
    sizes = (512, 512, 1024, 1024, GLA_GATE_RANK, 512, 512, 1024, 1024, ML_HEADS, ML_HEADS,
             GDN_QKV, 1024, GDN_HEADS, GDN_HEADS, 3 * D_MODEL)
    parts, o = [], 0
    for n in sizes:
        parts.append(w[:, o:o + n])
        o += n
    (gla_q, gla_k, gla_v, gla_r, gla_glr, ml_q, ml_k, ml_v, ml_o, ml_i, ml_f,
     gdn_qkv, gdn_z, gdn_a, gdn_b, merge) = parts
    main = jnp.concatenate([gdn_qkv, merge, gla_v, gla_r, ml_v, ml_o, gdn_z,
                            gla_q, gla_k, ml_q, ml_k], axis=1).astype(BF16)
    small = jnp.concatenate([gla_glr, ml_i, ml_f, gdn_a, gdn_b], axis=1)
    gate = jnp.pad(small, ((0, 0), (0, N_GATE - small.shape[1]))).astype(BF16)
    return main, gate


def _layout_w_up(w):
    d = w.shape[0]
    n = D_FF // FF_CHUNK
    g = w[:, :D_FF].reshape(d, n, FF_CHUNK)
    v = w[:, D_FF:].reshape(d, n, FF_CHUNK)
    return jnp.stack([g, v], axis=2).reshape(d, 2 * D_FF).astype(BF16)


def kernel(x, c, ada_w, ada_b, norm_mix, norm_ffn, w_in, gla_w_g2, gla_b_g2, gla_norm, ml_i_bias, ml_f_bias, ml_norm, gdn_conv, gdn_a_log, gdn_dt_bias, gdn_norm, w_out, w_ffn_up, w_ffn_down, norm_final):
    b, s, d = x.shape
    depth = ada_w.shape[0]
    mod = _adaln(c, ada_w, ada_b).reshape(depth, b, 6, d)
    x2 = x.reshape(b * s, d)
    fin = norm_final.reshape(1, d)
    for l in range(depth):
        w_main, w_gate = _layout_w_in(w_in[l])
        proj, gates = _inproj(x2, mod[l], norm_mix[l].reshape(1, d), w_main, w_gate, s)
        proj3 = proj.reshape(b, s, N_MAIN)
        gates3 = gates.reshape(b, s, N_GATE)
        y_gla = _gla(proj3, gates3, gla_w_g2[l], gla_b_g2[l], gla_norm[l])
        y_ml = _mlstm(proj3, gates3, ml_i_bias[l], ml_f_bias[l], ml_norm[l])
        y_gdn = _gdn(proj3, gates3, gdn_conv[l], gdn_a_log[l], gdn_dt_bias[l], gdn_norm[l])
        x2 = _merge(y_gla.reshape(b * s, d), y_ml.reshape(b * s, d), y_gdn.reshape(b * s, d),
                    proj, x2, mod[l], w_out[l].astype(BF16), s)
        x2 = _ffn(x2, mod[l], norm_ffn[l].reshape(1, d), _layout_w_up(w_ffn_up[l]),
                  w_ffn_down[l].astype(BF16), fin, s, final_norm=(l == depth - 1))
    return x2.reshape(b, s, d)
```

```python
import functools

import jax
import jax.numpy as jnp
from jax import lax
from jax.experimental import pallas as pl
from jax.experimental.pallas import tpu as pltpu

F32 = jnp.float32
BF16 = jnp.bfloat16

D_MODEL = 1024
NORM_EPS = 1e-6
GLA_HEADS, GLA_DK, GLA_DV = 4, 128, 256
GLA_QK = GLA_HEADS * GLA_DK
GLA_GATE_RANK = 16
GLA_GATE_NORM = 16.0
GLA_SUB = 16
ML_HEADS, ML_DK, ML_DV = 4, 128, 256
ML_QK = ML_HEADS * ML_DK
ML_GATE_CAP = 15.0
ML_M_INIT = -1e30
GDN_HEADS, GDN_DK, GDN_DV = 8, 128, 128
GDN_CONV = 4
GDN_QKV = GDN_HEADS * (2 * GDN_DK + GDN_DV)
D_FF = 2816
FF_CHUNK = 256

N_GATE = 128
LANE_GLR = 0
LANE_ML_I = 16
LANE_ML_F = 20
LANE_GDN_A = 24
LANE_GDN_B = 32

LANES = 128
SUBLANES = 8
VMEM_LIMIT = 56 * 1024 * 1024

MIX_BLOCK = 128
MIX_STEP = 256
TOK_TM = 512


def _dot(a, b):
    return jnp.dot(a, b, preferred_element_type=F32)


def _dot_nt(a, b):
    return lax.dot_general(a, b, (((1,), (1,)), ((), ())), preferred_element_type=F32)


def _dot_tn(a, b):
    return lax.dot_general(a, b, (((0,), (0,)), ((), ())), preferred_element_type=F32)


def _split3(x):
    hi = x.astype(BF16)
    r = x - hi.astype(F32)
    mid = r.astype(BF16)
    lo = (r - mid.astype(F32)).astype(BF16)
    return hi, mid, lo


def _dot_f32(a, b):
    a0, a1, a2 = _split3(a)
    b0, b1, b2 = _split3(b)
    small = _dot(a0, b2) + _dot(a2, b0) + _dot(a1, b1)
    return _dot(a0, b0) + (_dot(a0, b1) + _dot(a1, b0) + small)


def _cumsum_rows(tri, x):
    hi, mid, lo = _split3(x)
    return _dot(tri, hi) + (_dot(tri, mid) + _dot(tri, lo))


def _log_sigmoid(x):
    return jnp.minimum(x, 0.0) - jnp.log1p(jnp.exp(-jnp.abs(x)))


def _softplus(x):
    return jnp.maximum(x, 0.0) + jnp.log1p(jnp.exp(-jnp.abs(x)))


def _sigmoid(x):
    return 1.0 / (1.0 + jnp.exp(-x))


def _silu(x):
    return x * _sigmoid(x)


def _rms(x, gain):
    return x * lax.rsqrt(jnp.mean(x * x, axis=-1, keepdims=True) + NORM_EPS) * gain


def _modulated_norm(x, gain, shift, scale):
    return _rms(x, gain) * (1.0 + scale) + shift


def _params(*sem):
    return pltpu.CompilerParams(dimension_semantics=sem, vmem_limit_bytes=VMEM_LIMIT)


def _block_iotas(block):
    row = lax.broadcasted_iota(jnp.int32, (block, block), 0)
    col = lax.broadcasted_iota(jnp.int32, (block, block), 1)
    return row, col


def _adaln_kernel(c_ref, w_ref, b_ref, o_ref):
    c = c_ref[...]
    o_ref[...] = _dot_f32(_silu(c), w_ref[...]) + b_ref[...]


def _adaln(c, ada_w, ada_b):
    depth, d, n = ada_w.shape
    b = c.shape[0]
    tn = D_MODEL
    return pl.pallas_call(
        _adaln_kernel,
        grid=(depth, n // tn),
        in_specs=[
            pl.BlockSpec((b, d), lambda l, j: (0, 0)),
            pl.BlockSpec((None, d, tn), lambda l, j: (l, 0, j)),
            pl.BlockSpec((None, 1, tn), lambda l, j: (l, 0, j)),
        ],
        out_specs=pl.BlockSpec((None, b, tn), lambda l, j: (l, 0, j)),
        out_shape=jax.ShapeDtypeStruct((depth, b, n), F32),
        compiler_params=_params("arbitrary", "arbitrary"),
        name="adaln",
    )(c, ada_w, ada_b.reshape(depth, 1, n))


def _zero_state(state):
    for ref in state:
        ref[...] = jnp.zeros_like(ref)


def _mixer_kernel(x_ref, mod_ref, gain_ref, w_ref, wg_ref, *rest,
                  body, gates, reset, pre, n_params, row0, steps_per_seq):
    params = rest[:n_params]
    o_ref = rest[n_params]
    p_refs = rest[n_params + 1:n_params + 3]
    g_refs = rest[n_params + 3:n_params + 5]
    state = rest[n_params + 5:]
    s = pl.program_id(0)
    step = x_ref.shape[0]

    @pl.when(s == 0)
    def _():
        for ref in (*p_refs, *g_refs, *state):
            ref[...] = jnp.zeros_like(ref)

    t = (s + steps_per_seq - 1) % steps_per_seq

    @pl.when(t == 0)
    def _():
        reset(state)

    for parity in (0, 1):
        @pl.when(s % 2 == parity)
        def _(parity=parity):
            p_new, p_cur = p_refs[parity], p_refs[1 - parity]
            if pre is not None:
                pre(p_cur, p_new, t)
            gate_vals = gates(g_refs[1 - parity], params)
            h = _modulated_norm(x_ref[...], gain_ref[...], mod_ref[0:1, :], mod_ref[1:2, :]).astype(BF16)
            p_new[row0:row0 + step, :] = _dot(h, w_ref[...])
            g_refs[parity][...] = _dot(h, wg_ref[...])
            body(p_cur, gate_vals, params, o_ref, state)


def _mixer_call(name, gates, body, x2, mod, gain, w, w_gate, params, param_specs, state_shapes,
                seq, reset=_zero_state, pre=None, row0=0):
    t, d = x2.shape
    step = min(MIX_STEP, seq)
    n = t // step
    per_seq = seq // step
    n_cols = w.shape[1]

    def cur(s):
        return jnp.minimum(s, n - 1)

    return pl.pallas_call(
        functools.partial(_mixer_kernel, body=body, gates=gates, reset=reset, pre=pre,
                          n_params=len(params), row0=row0, steps_per_seq=per_seq),
        grid=(n + 1,),
        in_specs=[
            pl.BlockSpec((step, d), lambda s: (cur(s), 0)),
            pl.BlockSpec((None, 6, d), lambda s: (cur(s) // per_seq, 0, 0)),
            pl.BlockSpec((1, d), lambda s: (0, 0)),
            pl.BlockSpec((d, n_cols), lambda s: (0, 0)),
            pl.BlockSpec((d, N_GATE), lambda s: (0, 0)),
            *param_specs,
        ],
        out_specs=pl.BlockSpec((step, d), lambda s: (jnp.maximum(s - 1, 0), 0)),
        out_shape=jax.ShapeDtypeStruct((t, d), BF16),
        scratch_shapes=[
            pltpu.VMEM((row0 + step, n_cols), F32),
            pltpu.VMEM((row0 + step, n_cols), F32),
            pltpu.VMEM((step, N_GATE), F32),
            pltpu.VMEM((step, N_GATE), F32),
            *state_shapes,
        ],
        compiler_params=_params("arbitrary"),
        name=name,
    )(x2, mod, gain, w, w_gate, *params)


def _full_spec(shape):
    return pl.BlockSpec(shape, lambda s: (0,) * len(shape))


def _row_refs(c, block, seg, pick):
    parts = []
    for start in range(0, block, seg):
        r = pick(start)
        row = jnp.zeros((1, c.shape[1]), F32) if r is None else c[r:r + 1, :]
        parts.append(jnp.broadcast_to(row, (seg, c.shape[1])))
    return parts[0] if len(parts) == 1 else jnp.concatenate(parts, axis=0)


def _gla_gates(g_ref, params):
    wg_ref, bg_ref, _ = params
    step = g_ref.shape[0]
    block = min(MIX_BLOCK, step)
    row, col = _block_iotas(block)
    tri = (col <= row).astype(BF16)
    w0, w1, _ = _split3(wg_ref[...])
    cum = []
    for b in range(step // block):
        g0, g1, _ = _split3(g_ref[b * block:(b + 1) * block, :])
        x = _dot(g0, w0) + (_dot(g0, w1) + _dot(g1, w0)) + bg_ref[...]
        cum.append(_cumsum_rows(tri, _log_sigmoid(x) * (1.0 / GLA_GATE_NORM)))
    return cum


def _gla_body(p_ref, cum, params, o_ref, state):
    _, _, gain_ref = params
    (st_ref,) = state
    step = o_ref.shape[0]
    block = min(MIX_BLOCK, step)
    off_k, off_v, off_r = GLA_QK, 2 * GLA_QK, 2 * GLA_QK + D_MODEL

    row, col = _block_iotas(block)
    causal = col <= row
    sub_xor = (row // GLA_SUB) ^ (col // GLA_SUB)
    n_levels = (block // GLA_SUB).bit_length() - 1
    gain = gain_ref[...]
    blocks = range(step // block)
    units = [(b, h) for b in blocks for h in range(GLA_HEADS)]

    def rows(b, c0, width):
        return p_ref[b * block:(b + 1) * block, c0:c0 + width]

    c = {u: cum[u[0]][:, u[1] * GLA_DK:(u[1] + 1) * GLA_DK] for u in units}
    q = {u: rows(u[0], u[1] * GLA_DK, GLA_DK) * (GLA_DK ** -0.5) for u in units}
    k = {u: rows(u[0], off_k + u[1] * GLA_DK, GLA_DK) for u in units}
    v = {u: rows(u[0], off_v + u[1] * GLA_DV, GLA_DV).astype(BF16) for u in units}

    ref0 = {u: _row_refs(c[u], block, GLA_SUB, lambda s: None if s == 0 else s - 1) for u in units}
    p = {u: _dot_nt((q[u] * jnp.exp(c[u] - ref0[u])).astype(BF16),
                    (k[u] * jnp.exp(ref0[u] - c[u])).astype(BF16)) for u in units}
    a = {u: jnp.where(causal & (sub_xor == 0), p[u], 0.0) for u in units}
    for lev in range(1, n_levels + 1):
        half = GLA_SUB << (lev - 1)
        sel = causal & (sub_xor >= (1 << (lev - 1))) & (sub_xor < (1 << lev))
        ref = {u: _row_refs(c[u], block, 2 * half, lambda s: s + half - 1) for u in units}
        qs = {u: (q[u] * jnp.exp(jnp.minimum(c[u] - ref[u], 0.0))).astype(BF16) for u in units}
        ks = {u: (k[u] * jnp.exp(jnp.minimum(ref[u] - c[u], 0.0))).astype(BF16) for u in units}
        p = {u: _dot_nt(qs[u], ks[u]) for u in units}
        a = {u: jnp.where(sel, p[u], a[u]) for u in units}
    o_intra = {u: _dot(a[u].astype(BF16), v[u]) for u in units}
    q_dec = {u: (q[u] * jnp.exp(c[u])).astype(BF16) for u in units}
    c_end = {u: c[u][block - 1:block, :] for u in units}
    k_end = {u: (k[u] * jnp.exp(c_end[u] - c[u])).astype(BF16) for u in units}
    kv = {u: _dot_tn(v[u], k_end[u]) for u in units}

    st = [st_ref[h] for h in range(GLA_HEADS)]
    for b in blocks:
        heads = [(b, h) for h in range(GLA_HEADS)]
        o_inter = [_dot_nt(q_dec[u], st[u[1]].astype(BF16)) for u in heads]
        st = [st[u[1]] * jnp.exp(c_end[u]) + kv[u] for u in heads]
        for u, oi in zip(heads, o_inter):
            r = rows(b, off_r + u[1] * GLA_DV, GLA_DV)
            y = _rms(o_intra[u] + oi, gain) * _silu(r)
            o_ref[b * block:(b + 1) * block, u[1] * GLA_DV:(u[1] + 1) * GLA_DV] = y.astype(o_ref.dtype)
    for h in range(GLA_HEADS):
        st_ref[h] = st[h]


def _gla(x2, mod, gain, w, w_gate, w_g2, b_g2, norm_gain, seq):
    w_g2p = jnp.zeros((N_GATE, GLA_QK), F32).at[LANE_GLR:LANE_GLR + GLA_GATE_RANK].set(w_g2)
    params = (w_g2p, b_g2.reshape(1, GLA_QK), norm_gain.reshape(1, GLA_DV))
    specs = [_full_spec(p.shape) for p in params]
    state = [pltpu.VMEM((GLA_HEADS, GLA_DV, GLA_DK), F32)]
    return _mixer_call("gla", _gla_gates, _gla_body, x2, mod, gain, w, w_gate, params, specs, state,
                       seq)


def _mlstm_gates(g_ref, params):
    bias_ref, _ = params
    step = g_ref.shape[0]
    block = min(MIX_BLOCK, step)
    row, col = _block_iotas(block)
    tri = (col <= row).astype(BF16)
    log_i, log_i_t, cum, cum_t = [], [], [], []
    for b in range(step // block):
        capped = ML_GATE_CAP * jnp.tanh((g_ref[b * block:(b + 1) * block, :] + bias_ref[...])
                                        * (1.0 / ML_GATE_CAP))
        cs_b = _cumsum_rows(tri, _log_sigmoid(capped))
        log_i.append(capped)
        log_i_t.append(capped.T)
        cum.append(cs_b)
        cum_t.append(cs_b.T)
    return log_i, log_i_t, cum, cum_t


def _mlstm_body(p_ref, gate_vals, params, o_ref, state):
    _, gain_ref = params
    cs_ref, ns_ref, m_ref = state
    log_i, log_i_t, cum, cum_t = gate_vals
    step = o_ref.shape[0]
    block = min(MIX_BLOCK, step)
    off_k, off_v, off_o = ML_QK, 2 * ML_QK, 2 * ML_QK + D_MODEL

    row, col = _block_iotas(block)
    causal = col <= row
    gain = gain_ref[...]
    blocks = range(step // block)
    heads = range(ML_HEADS)
    units = [(b, h) for b in blocks for h in heads]

    def rows(b, c0, width):
        return p_ref[b * block:(b + 1) * block, c0:c0 + width]

    li_c = {u: log_i[u[0]][:, LANE_ML_I + u[1]:LANE_ML_I + u[1] + 1] for u in units}
    li_r = {u: log_i_t[u[0]][LANE_ML_I + u[1]:LANE_ML_I + u[1] + 1, :] for u in units}
    cum_c = {u: cum[u[0]][:, LANE_ML_F + u[1]:LANE_ML_F + u[1] + 1] for u in units}
    cum_r = {u: cum_t[u[0]][LANE_ML_F + u[1]:LANE_ML_F + u[1] + 1, :] for u in units}
    cum_last = {u: cum_c[u][block - 1:block, :] for u in units}
    log_end = {u: cum_last[u] - cum_c[u] + li_c[u] for u in units}
    m_prev, m_new = {}, {}
    for h in heads:
        m = m_ref[h:h + 1, 0:1]
        for b in blocks:
            u = (b, h)
            m_prev[u] = m
            m = jnp.maximum(cum_last[u] + m, jnp.max(log_end[u], axis=0, keepdims=True))
            m_new[u] = m
        m_ref[h:h + 1, :] = jnp.broadcast_to(m, (1, LANES))

    q = {u: rows(u[0], u[1] * ML_DK, ML_DK) * (ML_DK ** -0.5) for u in units}
    qb = {u: q[u].astype(BF16) for u in units}
    k = {u: rows(u[0], off_k + u[1] * ML_DK, ML_DK) for u in units}
    v = {u: rows(u[0], off_v + u[1] * ML_DV, ML_DV).astype(BF16) for u in units}
    qk = {u: _dot_nt(qb[u], k[u].astype(BF16)) for u in units}
    log_d = {u: jnp.where(causal, cum_c[u] - cum_r[u] + li_r[u], -jnp.inf) for u in units}
    m_inter = {u: cum_c[u] + m_prev[u] for u in units}
    m_t = {u: jnp.maximum(m_inter[u], jnp.max(log_d[u], axis=1, keepdims=True)) for u in units}
    a_inter = {u: jnp.exp(m_inter[u] - m_t[u]) for u in units}
    p = {u: qk[u] * jnp.exp(log_d[u] - m_t[u]) for u in units}
    pv = {u: _dot(p[u].astype(BF16), v[u]) for u in units}
    p_sum = {u: jnp.sum(p[u], axis=1, keepdims=True) for u in units}
    dec = {u: jnp.exp(cum_last[u] + m_prev[u] - m_new[u]) for u in units}
    kw = {u: k[u] * jnp.exp(log_end[u] - m_new[u]) for u in units}
    kwv = {u: _dot_tn(kw[u].astype(BF16), v[u]) for u in units}
    kw_sum = {u: jnp.sum(kw[u], axis=0, keepdims=True) for u in units}

    cs = [cs_ref[h] for h in heads]
    ns = [ns_ref[h:h + 1, :] for h in heads]
    for b in blocks:
        q_cs = [_dot(qb[(b, h)], cs[h].astype(BF16)) for h in heads]
        q_ns = [jnp.sum(q[(b, h)] * ns[h], axis=1, keepdims=True) for h in heads]
        cs = [dec[(b, h)] * cs[h] + kwv[(b, h)] for h in heads]
        ns = [dec[(b, h)] * ns[h] + kw_sum[(b, h)] for h in heads]
        for h in heads:
            u = (b, h)
            num = pv[u] + a_inter[u] * q_cs[h]
            den = p_sum[u] + a_inter[u] * q_ns[h]
            hid = num / jnp.maximum(jnp.abs(den), jnp.exp(-m_t[u]))
            og = rows(b, off_o + h * ML_DV, ML_DV)
            y = _rms(hid, gain) * _sigmoid(og)
            o_ref[b * block:(b + 1) * block, h * ML_DV:(h + 1) * ML_DV] = y.astype(o_ref.dtype)
    for h in heads:
        cs_ref[h] = cs[h]
        ns_ref[h:h + 1, :] = ns[h]


def _mlstm_reset(state):
    cs_ref, ns_ref, m_ref = state
    cs_ref[...] = jnp.zeros_like(cs_ref)
    ns_ref[...] = jnp.zeros_like(ns_ref)
    m_ref[...] = jnp.full_like(m_ref, ML_M_INIT)


def _mlstm(x2, mod, gain, w, w_gate, i_bias, f_bias, norm_gain, seq):
    bias = jnp.zeros((1, N_GATE), F32)
    bias = bias.at[0, LANE_ML_I:LANE_ML_I + ML_HEADS].set(i_bias)
    bias = bias.at[0, LANE_ML_F:LANE_ML_F + ML_HEADS].set(f_bias)
    params = (bias, norm_gain.reshape(1, ML_DV))
    specs = [_full_spec(p.shape) for p in params]
    state = [
        pltpu.VMEM((ML_HEADS, ML_DK, ML_DV), F32),
        pltpu.VMEM((SUBLANES, ML_DK), F32),
        pltpu.VMEM((SUBLANES, LANES), F32),
    ]
    return _mixer_call("mlstm", _mlstm_gates, _mlstm_body, x2, mod, gain, w, w_gate, params, specs,
                       state, seq, reset=_mlstm_reset)


def _gdn_pre(p_cur, p_new, t):
    step = p_cur.shape[0] - SUBLANES
    tail = p_new[step:step + SUBLANES, 0:GDN_QKV]
    p_cur[0:SUBLANES, 0:GDN_QKV] = jnp.where(t == 0, 0.0, tail)


def _gdn_gates(g_ref, params):
    _, prm_ref, _ = params
    step = g_ref.shape[0]
    block = min(MIX_BLOCK, step)
    row, col = _block_iotas(block)
    tri = (col <= row).astype(BF16)
    cum, cum_t, beta = [], [], []
    for b in range(step // block):
        g = g_ref[b * block:(b + 1) * block, :]
        decay = -jnp.exp(prm_ref[0:1, :]) * _softplus(g + prm_ref[1:2, :])
        cs_b = _cumsum_rows(tri, decay)
        cum.append(cs_b)
        cum_t.append(cs_b.T)
        beta.append(_sigmoid(g))
    return cum, cum_t, beta


def _gdn_body(p_ref, gate_vals, params, o_ref, state):
    conv_ref, _, gain_ref = params
    (s_ref,) = state
    cum, cum_t, beta = gate_vals
    step = o_ref.shape[0]
    block = min(MIX_BLOCK, step)
    off_z = GDN_QKV

    def conv_silu(b, c0):
        acc = None
        for j in range(GDN_CONV):
            r0 = SUBLANES + b * block - (GDN_CONV - 1) + j
            term = conv_ref[j:j + 1, c0:c0 + LANES] * p_ref[r0:r0 + block, c0:c0 + LANES]
            acc = term if acc is None else acc + term
        return _silu(acc)

    row, col = _block_iotas(block)
    causal = col <= row
    strict = col < row
    eye = (col == row).astype(F32)
    level_masks = [
        ((row >> j == col >> j) & (row >> (j - 1) != col >> (j - 1))).astype(F32)
        for j in range(1, block.bit_length())
    ]
    gain = gain_ref[...]
    blocks = range(step // block)
    heads = range(GDN_HEADS)
    units = [(b, h) for b in blocks for h in heads]

    cum_c = {u: cum[u[0]][:, LANE_GDN_A + u[1]:LANE_GDN_A + u[1] + 1] for u in units}
    cum_r = {u: cum_t[u[0]][LANE_GDN_A + u[1]:LANE_GDN_A + u[1] + 1, :] for u in units}
    beta_c = {u: beta[u[0]][:, LANE_GDN_B + u[1]:LANE_GDN_B + u[1] + 1] for u in units}
    q = {u: conv_silu(u[0], u[1] * GDN_DK) for u in units}
    k = {u: conv_silu(u[0], GDN_HEADS * GDN_DK + u[1] * GDN_DK) for u in units}
    v = {u: conv_silu(u[0], 2 * GDN_HEADS * GDN_DK + u[1] * GDN_DV) for u in units}
    q = {u: q[u] * lax.rsqrt(jnp.sum(q[u] * q[u], axis=-1, keepdims=True) + NORM_EPS) * (GDN_DK ** -0.5)
         for u in units}
    k = {u: k[u] * lax.rsqrt(jnp.sum(k[u] * k[u], axis=-1, keepdims=True) + NORM_EPS) for u in units}
    qb = {u: q[u].astype(BF16) for u in units}
    kb = {u: k[u].astype(BF16) for u in units}
    gamma = {u: jnp.where(causal, jnp.exp(jnp.minimum(cum_c[u] - cum_r[u], 0.0)), 0.0) for u in units}
    kk = {u: _dot_nt(kb[u], kb[u]) for u in units}
    a = {u: jnp.where(strict, beta_c[u] * kk[u] * gamma[u], 0.0) for u in units}
    inv = {u: eye - a[u] * level_masks[0] for u in units}
    for mask in level_masks[1:]:
        xb = {u: inv[u].astype(BF16) for u in units}
        xa = {u: _dot(xb[u], (a[u] * mask).astype(BF16)) for u in units}
        xax = {u: _dot(xa[u].astype(BF16), xb[u]) for u in units}
        inv = {u: inv[u] - xax[u] for u in units}

    e_cum = {u: jnp.exp(cum_c[u]) for u in units}
    rhs = {u: jnp.concatenate([v[u] * beta_c[u], k[u] * (beta_c[u] * e_cum[u])], axis=1).astype(BF16)
           for u in units}
    uw = {u: _dot(inv[u].astype(BF16), rhs[u]) for u in units}
    a_qk = {u: (_dot_nt(qb[u], kb[u]) * gamma[u]).astype(BF16) for u in units}
    q_dec = {u: (q[u] * e_cum[u]).astype(BF16) for u in units}
    cum_last = {u: cum_c[u][block - 1:block, :] for u in units}
    k_end = {u: (k[u] * jnp.exp(cum_last[u] - cum_c[u])).astype(BF16) for u in units}

    st = [s_ref[h] for h in heads]
    for b in blocks:
        us = [(b, h) for h in heads]
        sb = [t_.astype(BF16) for t_ in st]
        ws = [_dot(uw[u][:, GDN_DV:].astype(BF16), sb[u[1]]) for u in us]
        qs = [_dot(q_dec[u], sb[u[1]]) for u in us]
        vb = [(uw[u][:, :GDN_DV] - ws[u[1]]).astype(BF16) for u in us]
        o = [qs[u[1]] + _dot(a_qk[u], vb[u[1]]) for u in us]
        st = [st[u[1]] * jnp.exp(cum_last[u]) + _dot_tn(k_end[u], vb[u[1]]) for u in us]
        for u in us:
            h = u[1]
            z = p_ref[SUBLANES + b * block:SUBLANES + (b + 1) * block, off_z + h * GDN_DV:off_z + (h + 1) * GDN_DV]
            y = _rms(o[h], gain) * _silu(z)
            o_ref[b * block:(b + 1) * block, h * GDN_DV:(h + 1) * GDN_DV] = y.astype(o_ref.dtype)
    for h in heads:
        s_ref[h] = st[h]


def _gdn(x2, mod, gain, w, w_gate, conv_w, a_log, dt_bias, norm_gain, seq):
    prm = jnp.zeros((SUBLANES, N_GATE), F32)
    prm = prm.at[0, LANE_GDN_A:LANE_GDN_A + GDN_HEADS].set(a_log)
    prm = prm.at[1, LANE_GDN_A:LANE_GDN_A + GDN_HEADS].set(dt_bias)
    params = (conv_w, prm, norm_gain.reshape(1, GDN_DV))
    specs = [_full_spec(p.shape) for p in params]
    state = [pltpu.VMEM((GDN_HEADS, GDN_DK, GDN_DV), F32)]
    return _mixer_call("gdn", _gdn_gates, _gdn_body, x2, mod, gain, w, w_gate, params, specs, state,
                       seq, pre=_gdn_pre, row0=SUBLANES)


def _merge_kernel(yg_ref, ym_ref, yd_ref, x_ref, mod_ref, gain_ref, wm_ref, wo_ref, o_ref):
    x = x_ref[...]
    h = _modulated_norm(x, gain_ref[...], mod_ref[0:1, :], mod_ref[1:2, :]).astype(BF16)
    d = x.shape[1]
    y = None
    for i, y_ref in enumerate((yg_ref, ym_ref, yd_ref)):
        term = _sigmoid(_dot(h, wm_ref[:, i * d:(i + 1) * d])) * y_ref[...].astype(F32)
        y = term if y is None else y + term
    o_ref[...] = x + mod_ref[2:3, :] * _dot(y.astype(BF16), wo_ref[...])


def _merge(y_gla, y_ml, y_gdn, x2, mod, gain, w_merge, w_out, seq):
    t, d = x2.shape
    tm = min(TOK_TM, seq)
    per_seq = seq // tm
    tok = pl.BlockSpec((tm, d), lambda i: (i, 0))
    return pl.pallas_call(
        _merge_kernel,
        grid=(t // tm,),
        in_specs=[tok, tok, tok, tok,
                  pl.BlockSpec((None, 6, d), lambda i: (i // per_seq, 0, 0)),
                  pl.BlockSpec((1, d), lambda i: (0, 0)),
                  pl.BlockSpec((d, 3 * d), lambda i: (0, 0)),
                  pl.BlockSpec((d, d), lambda i: (0, 0))],
        out_specs=tok,
        out_shape=jax.ShapeDtypeStruct((t, d), F32),
        compiler_params=_params("arbitrary"),
        name="merge",
    )(y_gla, y_ml, y_gdn, x2, mod, gain, w_merge, w_out)


def _ffn_kernel(x_ref, mod_ref, gain_ref, wu_ref, wd_ref, fin_ref, o_ref, *, final_norm):
    x = x_ref[...]
    h = _modulated_norm(x, gain_ref[...], mod_ref[3:4, :], mod_ref[4:5, :]).astype(BF16)
    acc = None
    for c in range(D_FF // FF_CHUNK):
        gate = _dot(h, wu_ref[:, c * FF_CHUNK:(c + 1) * FF_CHUNK])
        val = _dot(h, wu_ref[:, D_FF + c * FF_CHUNK:D_FF + (c + 1) * FF_CHUNK])
        part = _dot((_silu(gate) * val).astype(BF16), wd_ref[c * FF_CHUNK:(c + 1) * FF_CHUNK, :])
        acc = part if acc is None else acc + part
    y = x + mod_ref[5:6, :] * acc
    if final_norm:
        y = _rms(y, fin_ref[...])
    o_ref[...] = y


def _ffn(x2, mod, gain, w_up, w_down, norm_final, seq, final_norm):
    t, d = x2.shape
    tm = min(TOK_TM, seq)
    per_seq = seq // tm
    tok = pl.BlockSpec((tm, d), lambda i: (i, 0))
    return pl.pallas_call(
        functools.partial(_ffn_kernel, final_norm=final_norm),
        grid=(t // tm,),
        in_specs=[tok,
                  pl.BlockSpec((None, 6, d), lambda i: (i // per_seq, 0, 0)),
                  pl.BlockSpec((1, d), lambda i: (0, 0)),
                  pl.BlockSpec((d, 2 * D_FF), lambda i: (0, 0)),
                  pl.BlockSpec((D_FF, d), lambda i: (0, 0)),
                  pl.BlockSpec((1, d), lambda i: (0, 0))],
        out_specs=tok,
        out_shape=jax.ShapeDtypeStruct((t, d), F32),
        compiler_params=_params("arbitrary"),
        name="ffn",
    )(x2, mod, gain, w_up, w_down, norm_final)


def _layout_w_in(w):
    sizes = (GLA_QK, GLA_QK, D_MODEL, D_MODEL, GLA_GATE_RANK, ML_QK, ML_QK, D_MODEL, D_MODEL,
             ML_HEADS, ML_HEADS, GDN_QKV, D_MODEL, GDN_HEADS, GDN_HEADS, 3 * D_MODEL)
    parts, o = [], 0
    for n in sizes:
        parts.append(w[:, o:o + n])
        o += n
    (gla_q, gla_k, gla_v, gla_r, gla_glr, ml_q, ml_k, ml_v, ml_o, ml_i, ml_f,
     gdn_qkv, gdn_z, gdn_a, gdn_b, merge) = parts

    def cat(*ts):
        return jnp.concatenate(ts, axis=1).astype(BF16)

    small = jnp.concatenate([gla_glr, ml_i, ml_f, gdn_a, gdn_b], axis=1)
    gate = jnp.pad(small, ((0, 0), (0, N_GATE - small.shape[1]))).astype(BF16)
    return (cat(gla_q, gla_k, gla_v, gla_r), cat(ml_q, ml_k, ml_v, ml_o), cat(gdn_qkv, gdn_z),
            merge.astype(BF16), gate)


def kernel(x, c, ada_w, ada_b, norm_mix, norm_ffn, w_in, gla_w_g2, gla_b_g2, gla_norm, ml_i_bias, ml_f_bias, ml_norm, gdn_conv, gdn_a_log, gdn_dt_bias, gdn_norm, w_out, w_ffn_up, w_ffn_down, norm_final):
    b, s, d = x.shape
    depth = ada_w.shape[0]
    mod = _adaln(c, ada_w, ada_b).reshape(depth, b, 6, d)
    x2 = x.reshape(b * s, d)
    fin = norm_final.reshape(1, d)
    for l in range(depth):
        w_gla, w_ml, w_gdn, w_merge, w_gate = _layout_w_in(w_in[l])
        gain = norm_mix[l].reshape(1, d)
        y_gla = _gla(x2, mod[l], gain, w_gla, w_gate, gla_w_g2[l], gla_b_g2[l], gla_norm[l], s)
        y_ml = _mlstm(x2, mod[l], gain, w_ml, w_gate, ml_i_bias[l], ml_f_bias[l], ml_norm[l], s)
        y_gdn = _gdn(x2, mod[l], gain, w_gdn, w_gate, gdn_conv[l], gdn_a_log[l], gdn_dt_bias[l],
                     gdn_norm[l], s)
        x2 = _merge(y_gla, y_ml, y_gdn, x2, mod[l], gain, w_merge, w_out[l].astype(BF16), s)
        x2 = _ffn(x2, mod[l], norm_ffn[l].reshape(1, d), w_ffn_up[l].astype(BF16),
                  w_ffn_down[l].astype(BF16), fin, s, final_norm=(l == depth - 1))
    return x2.reshape(b, s, d)
```

```python
import functools

import jax
import jax.numpy as jnp
from jax import lax
from jax.experimental import pallas as pl
from jax.experimental.pallas import tpu as pltpu

F32 = jnp.float32
BF16 = jnp.bfloat16

D_MODEL = 1024
NORM_EPS = 1e-6
GLA_HEADS, GLA_DK, GLA_DV = 4, 128, 256
GLA_QK = GLA_HEADS * GLA_DK
GLA_GATE_RANK = 16
GLA_GATE_NORM = 16.0
GLA_SUB = 16
ML_HEADS, ML_DK, ML_DV = 4, 128, 256
ML_QK = ML_HEADS * ML_DK
ML_GATE_CAP = 15.0
ML_M_INIT = -1e30
GDN_HEADS, GDN_DK, GDN_DV = 8, 128, 128
GDN_CONV = 4
GDN_QKV = GDN_HEADS * (2 * GDN_DK + GDN_DV)
D_FF = 2816
FF_CHUNK = 256

N_GATE = 128
LANE_GLR = 0
LANE_ML_I = 16
LANE_ML_F = 20
LANE_GDN_A = 24
LANE_GDN_B = 32

LANES = 128
SUBLANES = 8
VMEM_LIMIT = 56 * 1024 * 1024

MIX_BLOCK = 128
MIX_STEP = 256
TOK_TM = 512


def _dot(a, b):
    return jnp.dot(a, b, preferred_element_type=F32)


def _dot_nt(a, b):
    return lax.dot_general(a, b, (((1,), (1,)), ((), ())), preferred_element_type=F32)


def _dot_tn(a, b):
    return lax.dot_general(a, b, (((0,), (0,)), ((), ())), preferred_element_type=F32)


def _split3(x):
    hi = x.astype(BF16)
    r = x - hi.astype(F32)
    mid = r.astype(BF16)
    lo = (r - mid.astype(F32)).astype(BF16)
    return hi, mid, lo


def _dot_f32(a, b):
    a0, a1, a2 = _split3(a)
    b0, b1, b2 = _split3(b)
    small = _dot(a0, b2) + _dot(a2, b0) + _dot(a1, b1)
    return _dot(a0, b0) + (_dot(a0, b1) + _dot(a1, b0) + small)


def _cumsum_rows(tri, x):
    hi, mid, lo = _split3(x)
    return _dot(tri, hi) + (_dot(tri, mid) + _dot(tri, lo))


def _log_sigmoid(x):
    return jnp.minimum(x, 0.0) - jnp.log1p(jnp.exp(-jnp.abs(x)))


def _softplus(x):
    return jnp.maximum(x, 0.0) + jnp.log1p(jnp.exp(-jnp.abs(x)))


def _sigmoid(x):
    return 1.0 / (1.0 + jnp.exp(-x))


def _silu(x):
    return x * _sigmoid(x)


def _rms(x, gain):
    return x * lax.rsqrt(jnp.mean(x * x, axis=-1, keepdims=True) + NORM_EPS) * gain


def _modulated_norm(x, gain, shift, scale):
    return _rms(x, gain) * (1.0 + scale) + shift


def _params(*sem):
    return pltpu.CompilerParams(dimension_semantics=sem, vmem_limit_bytes=VMEM_LIMIT)


def _block_iotas(block):
    row = lax.broadcasted_iota(jnp.int32, (block, block), 0)
    col = lax.broadcasted_iota(jnp.int32, (block, block), 1)
    return row, col


def _adaln_kernel(c_ref, w_ref, b_ref, o_ref):
    c = c_ref[...]
    o_ref[...] = _dot_f32(_silu(c), w_ref[...]) + b_ref[...]


def _adaln(c, ada_w, ada_b):
    depth, d, n = ada_w.shape
    b = c.shape[0]
    tn = D_MODEL
    return pl.pallas_call(
        _adaln_kernel,
        grid=(depth, n // tn),
        in_specs=[
            pl.BlockSpec((b, d), lambda l, j: (0, 0)),
            pl.BlockSpec((None, d, tn), lambda l, j: (l, 0, j)),
            pl.BlockSpec((None, 1, tn), lambda l, j: (l, 0, j)),
        ],
        out_specs=pl.BlockSpec((None, b, tn), lambda l, j: (l, 0, j)),
        out_shape=jax.ShapeDtypeStruct((depth, b, n), F32),
        compiler_params=_params("arbitrary", "arbitrary"),
        name="adaln",
    )(c, ada_w, ada_b.reshape(depth, 1, n))


def _zero_state(state):
    for ref in state:
        ref[...] = jnp.zeros_like(ref)


def _mixer_kernel(x_ref, mod_ref, gain_ref, w_ref, wg_ref, *rest,
                  body, gates, reset, pre, n_params, row0, steps_per_seq):
    params = rest[:n_params]
    o_ref = rest[n_params]
    p_refs = rest[n_params + 1:n_params + 3]
    g_refs = rest[n_params + 3:n_params + 5]
    state = rest[n_params + 5:]
    s = pl.program_id(0)
    step = x_ref.shape[0]

    @pl.when(s == 0)
    def _():
        for ref in (*p_refs, *g_refs, *state):
            ref[...] = jnp.zeros_like(ref)

    t = (s + steps_per_seq - 1) % steps_per_seq

    @pl.when(t == 0)
    def _():
        reset(state)

    for parity in (0, 1):
        @pl.when(s % 2 == parity)
        def _(parity=parity):
            p_new, p_cur = p_refs[parity], p_refs[1 - parity]
            if pre is not None:
                pre(p_cur, p_new, t)
            gate_vals = gates(g_refs[1 - parity], params)
            h = _modulated_norm(x_ref[...], gain_ref[...], mod_ref[0:1, :], mod_ref[1:2, :]).astype(BF16)
            p_new[row0:row0 + step, :] = _dot(h, w_ref[...])
            g_refs[parity][...] = _dot(h, wg_ref[...])
            body(p_cur, gate_vals, params, o_ref, state)


def _mixer_call(name, gates, body, x2, mod, gain, w, w_gate, params, param_specs, state_shapes,
                seq, reset=_zero_state, pre=None, row0=0):
    t, d = x2.shape
    step = min(MIX_STEP, seq)
    n = t // step
    per_seq = seq // step
    n_cols = w.shape[1]

    def cur(s):
        return jnp.minimum(s, n - 1)

    return pl.pallas_call(
        functools.partial(_mixer_kernel, body=body, gates=gates, reset=reset, pre=pre,
                          n_params=len(params), row0=row0, steps_per_seq=per_seq),
        grid=(n + 1,),
        in_specs=[
            pl.BlockSpec((step, d), lambda s: (cur(s), 0)),
            pl.BlockSpec((None, 6, d), lambda s: (cur(s) // per_seq, 0, 0)),
            pl.BlockSpec((1, d), lambda s: (0, 0)),
            pl.BlockSpec((d, n_cols), lambda s: (0, 0)),
            pl.BlockSpec((d, N_GATE), lambda s: (0, 0)),
            *param_specs,
        ],
        out_specs=pl.BlockSpec((step, d), lambda s: (jnp.maximum(s - 1, 0), 0)),
        out_shape=jax.ShapeDtypeStruct((t, d), BF16),
        scratch_shapes=[
            pltpu.VMEM((row0 + step, n_cols), F32),
            pltpu.VMEM((row0 + step, n_cols), F32),
            pltpu.VMEM((step, N_GATE), F32),
            pltpu.VMEM((step, N_GATE), F32),
            *state_shapes,
        ],
        compiler_params=_params("arbitrary"),
        name=name,
    )(x2, mod, gain, w, w_gate, *params)


def _full_spec(shape):
    return pl.BlockSpec(shape, lambda s: (0,) * len(shape))


def _row_refs(c, block, seg, pick):
    parts = []
    for start in range(0, block, seg):
        r = pick(start)
        row = jnp.zeros((1, c.shape[1]), F32) if r is None else c[r:r + 1, :]
        parts.append(jnp.broadcast_to(row, (seg, c.shape[1])))
    return parts[0] if len(parts) == 1 else jnp.concatenate(parts, axis=0)


def _gla_gates(g_ref, params):
    wg_ref, bg_ref, _ = params
    step = g_ref.shape[0]
    block = min(MIX_BLOCK, step)
    row, col = _block_iotas(block)
    tri = (col <= row).astype(BF16)
    w0, w1, _ = _split3(wg_ref[...])
    cum = []
    for b in range(step // block):
        g0, g1, _ = _split3(g_ref[b * block:(b + 1) * block, :])
        x = _dot(g0, w0) + (_dot(g0, w1) + _dot(g1, w0)) + bg_ref[...]
        cum.append(_cumsum_rows(tri, _log_sigmoid(x) * (1.0 / GLA_GATE_NORM)))
    return cum


def _gla_body(p_ref, cum, params, o_ref, state):
    _, _, gain_ref = params
    (st_ref,) = state
    step = o_ref.shape[0]
    block = min(MIX_BLOCK, step)
    off_k, off_v, off_r = GLA_QK, 2 * GLA_QK, 2 * GLA_QK + D_MODEL

    row, col = _block_iotas(block)
    causal = col <= row
    sub_xor = (row // GLA_SUB) ^ (col // GLA_SUB)
    n_levels = (block // GLA_SUB).bit_length() - 1
    gain = gain_ref[...]
    blocks = range(step // block)
    units = [(b, h) for b in blocks for h in range(GLA_HEADS)]

    def rows(b, c0, width):
        return p_ref[b * block:(b + 1) * block, c0:c0 + width]

    c = {u: cum[u[0]][:, u[1] * GLA_DK:(u[1] + 1) * GLA_DK] for u in units}
    q = {u: rows(u[0], u[1] * GLA_DK, GLA_DK) * (GLA_DK ** -0.5) for u in units}
    k = {u: rows(u[0], off_k + u[1] * GLA_DK, GLA_DK) for u in units}
    v = {u: rows(u[0], off_v + u[1] * GLA_DV, GLA_DV).astype(BF16) for u in units}

    ref0 = {u: _row_refs(c[u], block, GLA_SUB, lambda s: None if s == 0 else s - 1) for u in units}
    p = {u: _dot_nt((q[u] * jnp.exp(c[u] - ref0[u])).astype(BF16),
                    (k[u] * jnp.exp(ref0[u] - c[u])).astype(BF16)) for u in units}
    a = {u: jnp.where(causal & (sub_xor == 0), p[u], 0.0) for u in units}
    for lev in range(1, n_levels + 1):
        half = GLA_SUB << (lev - 1)
        sel = causal & (sub_xor >= (1 << (lev - 1))) & (sub_xor < (1 << lev))
        ref = {u: _row_refs(c[u], block, 2 * half, lambda s: s + half - 1) for u in units}
        qs = {u: (q[u] * jnp.exp(jnp.minimum(c[u] - ref[u], 0.0))).astype(BF16) for u in units}
        ks = {u: (k[u] * jnp.exp(jnp.minimum(ref[u] - c[u], 0.0))).astype(BF16) for u in units}
        p = {u: _dot_nt(qs[u], ks[u]) for u in units}
        a = {u: jnp.where(sel, p[u], a[u]) for u in units}
    o_intra = {u: _dot(a[u].astype(BF16), v[u]) for u in units}
    q_dec = {u: (q[u] * jnp.exp(c[u])).astype(BF16) for u in units}
    c_end = {u: c[u][block - 1:block, :] for u in units}
    k_end = {u: (k[u] * jnp.exp(c_end[u] - c[u])).astype(BF16) for u in units}
    kv = {u: _dot_tn(v[u], k_end[u]) for u in units}

    st = [st_ref[h] for h in range(GLA_HEADS)]
    for b in blocks:
        heads = [(b, h) for h in range(GLA_HEADS)]
        o_inter = [_dot_nt(q_dec[u], st[u[1]].astype(BF16)) for u in heads]
        st = [st[u[1]] * jnp.exp(c_end[u]) + kv[u] for u in heads]
        for u, oi in zip(heads, o_inter):
            r = rows(b, off_r + u[1] * GLA_DV, GLA_DV)
            y = _rms(o_intra[u] + oi, gain) * _silu(r)
            o_ref[b * block:(b + 1) * block, u[1] * GLA_DV:(u[1] + 1) * GLA_DV] = y.astype(o_ref.dtype)
    for h in range(GLA_HEADS):
        st_ref[h] = st[h]


def _gla(x2, mod, gain, w, w_gate, w_g2, b_g2, norm_gain, seq):
    w_g2p = jnp.zeros((N_GATE, GLA_QK), F32).at[LANE_GLR:LANE_GLR + GLA_GATE_RANK].set(w_g2)
    params = (w_g2p, b_g2.reshape(1, GLA_QK), norm_gain.reshape(1, GLA_DV))
    specs = [_full_spec(p.shape) for p in params]
    state = [pltpu.VMEM((GLA_HEADS, GLA_DV, GLA_DK), F32)]
    return _mixer_call("gla", _gla_gates, _gla_body, x2, mod, gain, w, w_gate, params, specs, state,
                       seq)


def _mlstm_gates(g_ref, params):
    bias_ref, _ = params
    step = g_ref.shape[0]
    block = min(MIX_BLOCK, step)
    row, col = _block_iotas(block)
    tri = (col <= row).astype(BF16)
    log_i, log_i_t, cum, cum_t = [], [], [], []
    for b in range(step // block):
        capped = ML_GATE_CAP * jnp.tanh((g_ref[b * block:(b + 1) * block, :] + bias_ref[...])
                                        * (1.0 / ML_GATE_CAP))
        cs_b = _cumsum_rows(tri, _log_sigmoid(capped))
        log_i.append(capped)
        log_i_t.append(capped.T)
        cum.append(cs_b)
        cum_t.append(cs_b.T)
    return log_i, log_i_t, cum, cum_t


def _mlstm_body(p_ref, gate_vals, params, o_ref, state):
    _, gain_ref = params
    cs_ref, ns_ref, m_ref = state
    log_i, log_i_t, cum, cum_t = gate_vals
    step = o_ref.shape[0]
    block = min(MIX_BLOCK, step)
    off_k, off_v, off_o = ML_QK, 2 * ML_QK, 2 * ML_QK + D_MODEL

    row, col = _block_iotas(block)
    causal = col <= row
    gain = gain_ref[...]
    blocks = range(step // block)
    heads = range(ML_HEADS)
    units = [(b, h) for b in blocks for h in heads]

    def rows(b, c0, width):
        return p_ref[b * block:(b + 1) * block, c0:c0 + width]

    li_c = {u: log_i[u[0]][:, LANE_ML_I + u[1]:LANE_ML_I + u[1] + 1] for u in units}
    li_r = {u: log_i_t[u[0]][LANE_ML_I + u[1]:LANE_ML_I + u[1] + 1, :] for u in units}
    cum_c = {u: cum[u[0]][:, LANE_ML_F + u[1]:LANE_ML_F + u[1] + 1] for u in units}
    cum_r = {u: cum_t[u[0]][LANE_ML_F + u[1]:LANE_ML_F + u[1] + 1, :] for u in units}
    cum_last = {u: cum_c[u][block - 1:block, :] for u in units}
    log_end = {u: cum_last[u] - cum_c[u] + li_c[u] for u in units}
    m_prev, m_new = {}, {}
    for h in heads:
        m = m_ref[h:h + 1, 0:1]
        for b in blocks:
            u = (b, h)
            m_prev[u] = m
            m = jnp.maximum(cum_last[u] + m, jnp.max(log_end[u], axis=0, keepdims=True))
            m_new[u] = m
        m_ref[h:h + 1, :] = jnp.broadcast_to(m, (1, LANES))

    q = {u: rows(u[0], u[1] * ML_DK, ML_DK) * (ML_DK ** -0.5) for u in units}
    qb = {u: q[u].astype(BF16) for u in units}
    k = {u: rows(u[0], off_k + u[1] * ML_DK, ML_DK) for u in units}
    v = {u: rows(u[0], off_v + u[1] * ML_DV, ML_DV).astype(BF16) for u in units}
    qk = {u: _dot_nt(qb[u], k[u].astype(BF16)) for u in units}
    log_d = {u: jnp.where(causal, cum_c[u] - cum_r[u] + li_r[u], -jnp.inf) for u in units}
    m_inter = {u: cum_c[u] + m_prev[u] for u in units}
    m_t = {u: jnp.maximum(m_inter[u], jnp.max(log_d[u], axis=1, keepdims=True)) for u in units}
    a_inter = {u: jnp.exp(m_inter[u] - m_t[u]) for u in units}
    p = {u: qk[u] * jnp.exp(log_d[u] - m_t[u]) for u in units}
    pv = {u: _dot(p[u].astype(BF16), v[u]) for u in units}
    p_sum = {u: jnp.sum(p[u], axis=1, keepdims=True) for u in units}
    dec = {u: jnp.exp(cum_last[u] + m_prev[u] - m_new[u]) for u in units}
    kw = {u: k[u] * jnp.exp(log_end[u] - m_new[u]) for u in units}
    kwv = {u: _dot_tn(kw[u].astype(BF16), v[u]) for u in units}
    kw_sum = {u: jnp.sum(kw[u], axis=0, keepdims=True) for u in units}

    cs = [cs_ref[h] for h in heads]
    ns = [ns_ref[h:h + 1, :] for h in heads]
    for b in blocks:
        q_cs = [_dot(qb[(b, h)], cs[h].astype(BF16)) for h in heads]
        q_ns = [jnp.sum(q[(b, h)] * ns[h], axis=1, keepdims=True) for h in heads]
        cs = [dec[(b, h)] * cs[h] + kwv[(b, h)] for h in heads]
        ns = [dec[(b, h)] * ns[h] + kw_sum[(b, h)] for h in heads]
        for h in heads:
            u = (b, h)
            num = pv[u] + a_inter[u] * q_cs[h]
            den = p_sum[u] + a_inter[u] * q_ns[h]
            hid = num / jnp.maximum(jnp.abs(den), jnp.exp(-m_t[u]))
            og = rows(b, off_o + h * ML_DV, ML_DV)
            y = _rms(hid, gain) * _sigmoid(og)
            o_ref[b * block:(b + 1) * block, h * ML_DV:(h + 1) * ML_DV] = y.astype(o_ref.dtype)
    for h in heads:
        cs_ref[h] = cs[h]
        ns_ref[h:h + 1, :] = ns[h]


def _mlstm_reset(state):
    cs_ref, ns_ref, m_ref = state
    cs_ref[...] = jnp.zeros_like(cs_ref)
    ns_ref[...] = jnp.zeros_like(ns_ref)
    m_ref[...] = jnp.full_like(m_ref, ML_M_INIT)


def _mlstm(x2, mod, gain, w, w_gate, i_bias, f_bias, norm_gain, seq):
    bias = jnp.zeros((1, N_GATE), F32)
    bias = bias.at[0, LANE_ML_I:LANE_ML_I + ML_HEADS].set(i_bias)
    bias = bias.at[0, LANE_ML_F:LANE_ML_F + ML_HEADS].set(f_bias)
    params = (bias, norm_gain.reshape(1, ML_DV))
    specs = [_full_spec(p.shape) for p in params]
    state = [
        pltpu.VMEM((ML_HEADS, ML_DK, ML_DV), F32),
        pltpu.VMEM((SUBLANES, ML_DK), F32),
        pltpu.VMEM((SUBLANES, LANES), F32),
    ]
    return _mixer_call("mlstm", _mlstm_gates, _mlstm_body, x2, mod, gain, w, w_gate, params, specs,
                       state, seq, reset=_mlstm_reset)


def _gdn_pre(p_cur, p_new, t):
    step = p_cur.shape[0] - SUBLANES
    tail = p_new[step:step + SUBLANES, 0:GDN_QKV]
    p_cur[0:SUBLANES, 0:GDN_QKV] = jnp.where(t == 0, 0.0, tail)


def _gdn_gates(g_ref, params):
    _, prm_ref, _ = params
    step = g_ref.shape[0]
    block = min(MIX_BLOCK, step)
    row, col = _block_iotas(block)
    tri = (col <= row).astype(BF16)
    cum, cum_t, beta = [], [], []
    for b in range(step // block):
        g = g_ref[b * block:(b + 1) * block, :]
        decay = -jnp.exp(prm_ref[0:1, :]) * _softplus(g + prm_ref[1:2, :])
        cs_b = _cumsum_rows(tri, decay)
        cum.append(cs_b)
        cum_t.append(cs_b.T)
        beta.append(_sigmoid(g))
    return cum, cum_t, beta


def _gdn_body(p_ref, gate_vals, params, o_ref, state):
    conv_ref, _, gain_ref = params
    (s_ref,) = state
    cum, cum_t, beta = gate_vals
    step = o_ref.shape[0]
    block = min(MIX_BLOCK, step)
    off_z = GDN_QKV

    def conv_silu(b, c0):
        acc = None
        for j in range(GDN_CONV):
            r0 = SUBLANES + b * block - (GDN_CONV - 1) + j
            term = conv_ref[j:j + 1, c0:c0 + LANES] * p_ref[r0:r0 + block, c0:c0 + LANES]
            acc = term if acc is None else acc + term
        return _silu(acc)

    row, col = _block_iotas(block)
    causal = col <= row
    strict = col < row
    eye = (col == row).astype(F32)
    level_masks = [
        ((row >> j == col >> j) & (row >> (j - 1) != col >> (j - 1))).astype(F32)
        for j in range(1, block.bit_length())
    ]
    gain = gain_ref[...]
    blocks = range(step // block)
    heads = range(GDN_HEADS)
    units = [(b, h) for b in blocks for h in heads]

    cum_c = {u: cum[u[0]][:, LANE_GDN_A + u[1]:LANE_GDN_A + u[1] + 1] for u in units}
    cum_r = {u: cum_t[u[0]][LANE_GDN_A + u[1]:LANE_GDN_A + u[1] + 1, :] for u in units}
    beta_c = {u: beta[u[0]][:, LANE_GDN_B + u[1]:LANE_GDN_B + u[1] + 1] for u in units}
    q = {u: conv_silu(u[0], u[1] * GDN_DK) for u in units}
    k = {u: conv_silu(u[0], GDN_HEADS * GDN_DK + u[1] * GDN_DK) for u in units}
    v = {u: conv_silu(u[0], 2 * GDN_HEADS * GDN_DK + u[1] * GDN_DV) for u in units}
    q = {u: q[u] * lax.rsqrt(jnp.sum(q[u] * q[u], axis=-1, keepdims=True) + NORM_EPS) * (GDN_DK ** -0.5)
         for u in units}
    k = {u: k[u] * lax.rsqrt(jnp.sum(k[u] * k[u], axis=-1, keepdims=True) + NORM_EPS) for u in units}
    qb = {u: q[u].astype(BF16) for u in units}
    kb = {u: k[u].astype(BF16) for u in units}
    gamma = {u: jnp.where(causal, jnp.exp(jnp.minimum(cum_c[u] - cum_r[u], 0.0)), 0.0) for u in units}
    kk = {u: _dot_nt(kb[u], kb[u]) for u in units}
    a = {u: jnp.where(strict, beta_c[u] * kk[u] * gamma[u], 0.0) for u in units}
    inv = {u: eye - a[u] * level_masks[0] for u in units}
    for mask in level_masks[1:]:
        xb = {u: inv[u].astype(BF16) for u in units}
        xa = {u: _dot(xb[u], (a[u] * mask).astype(BF16)) for u in units}
        xax = {u: _dot(xa[u].astype(BF16), xb[u]) for u in units}
        inv = {u: inv[u] - xax[u] for u in units}

    e_cum = {u: jnp.exp(cum_c[u]) for u in units}
    rhs = {u: jnp.concatenate([v[u] * beta_c[u], k[u] * (beta_c[u] * e_cum[u])], axis=1).astype(BF16)
           for u in units}
    uw = {u: _dot(inv[u].astype(BF16), rhs[u]) for u in units}
    a_qk = {u: (_dot_nt(qb[u], kb[u]) * gamma[u]).astype(BF16) for u in units}
    q_dec = {u: (q[u] * e_cum[u]).astype(BF16) for u in units}
    cum_last = {u: cum_c[u][block - 1:block, :] for u in units}
    k_end = {u: (k[u] * jnp.exp(cum_last[u] - cum_c[u])).astype(BF16) for u in units}

    st = [s_ref[h] for h in heads]
    for b in blocks:
        us = [(b, h) for h in heads]
        sb = [t_.astype(BF16) for t_ in st]
        ws = [_dot(uw[u][:, GDN_DV:].astype(BF16), sb[u[1]]) for u in us]
        qs = [_dot(q_dec[u], sb[u[1]]) for u in us]
        vb = [(uw[u][:, :GDN_DV] - ws[u[1]]).astype(BF16) for u in us]
        o = [qs[u[1]] + _dot(a_qk[u], vb[u[1]]) for u in us]
        st = [st[u[1]] * jnp.exp(cum_last[u]) + _dot_tn(k_end[u], vb[u[1]]) for u in us]
        for u in us:
            h = u[1]
            z = p_ref[SUBLANES + b * block:SUBLANES + (b + 1) * block, off_z + h * GDN_DV:off_z + (h + 1) * GDN_DV]
            y = _rms(o[h], gain) * _silu(z)
            o_ref[b * block:(b + 1) * block, h * GDN_DV:(h + 1) * GDN_DV] = y.astype(o_ref.dtype)
    for h in heads:
        s_ref[h] = st[h]


def _gdn(x2, mod, gain, w, w_gate, conv_w, a_log, dt_bias, norm_gain, seq):
    prm = jnp.zeros((SUBLANES, N_GATE), F32)
    prm = prm.at[0, LANE_GDN_A:LANE_GDN_A + GDN_HEADS].set(a_log)
    prm = prm.at[1, LANE_GDN_A:LANE_GDN_A + GDN_HEADS].set(dt_bias)
    params = (conv_w, prm, norm_gain.reshape(1, GDN_DV))
    specs = [_full_spec(p.shape) for p in params]
    state = [pltpu.VMEM((GDN_HEADS, GDN_DK, GDN_DV), F32)]
    return _mixer_call("gdn", _gdn_gates, _gdn_body, x2, mod, gain, w, w_gate, params, specs, state,
                       seq, pre=_gdn_pre, row0=SUBLANES)


def _tail_kernel(yg_ref, ym_ref, yd_ref, x_ref, mod_ref, gmix_ref, gffn_ref, wm_ref, wo_ref, wu_ref,
                 wd_ref, fin_ref, o_ref, *, final_norm):
    x = x_ref[...]
    d = x.shape[1]
    h = _modulated_norm(x, gmix_ref[...], mod_ref[0:1, :], mod_ref[1:2, :]).astype(BF16)
    y = None
    for i, y_ref in enumerate((yg_ref, ym_ref, yd_ref)):
        term = _sigmoid(_dot(h, wm_ref[:, i * d:(i + 1) * d])) * y_ref[...].astype(F32)
        y = term if y is None else y + term
    x = x + mod_ref[2:3, :] * _dot(y.astype(BF16), wo_ref[...])

    h = _modulated_norm(x, gffn_ref[...], mod_ref[3:4, :], mod_ref[4:5, :]).astype(BF16)
    acc = None
    for c in range(D_FF // FF_CHUNK):
        gate = _dot(h, wu_ref[:, c * FF_CHUNK:(c + 1) * FF_CHUNK])
        val = _dot(h, wu_ref[:, D_FF + c * FF_CHUNK:D_FF + (c + 1) * FF_CHUNK])
        part = _dot((_silu(gate) * val).astype(BF16), wd_ref[c * FF_CHUNK:(c + 1) * FF_CHUNK, :])
        acc = part if acc is None else acc + part
    y = x + mod_ref[5:6, :] * acc
    if final_norm:
        y = _rms(y, fin_ref[...])
    o_ref[...] = y


def _tail(y_gla, y_ml, y_gdn, x2, mod, gain_mix, gain_ffn, w_merge, w_out, w_up, w_down, norm_final,
          seq, final_norm):
    t, d = x2.shape
    tm = min(TOK_TM, seq)
    per_seq = seq // tm
    tok = pl.BlockSpec((tm, d), lambda i: (i, 0))

    def resident(shape):
        return pl.BlockSpec(shape, lambda i: (0, 0), pipeline_mode=pl.Buffered(1))

    return pl.pallas_call(
        functools.partial(_tail_kernel, final_norm=final_norm),
        grid=(t // tm,),
        in_specs=[tok, tok, tok, tok,
                  pl.BlockSpec((None, 6, d), lambda i: (i // per_seq, 0, 0)),
                  resident((1, d)), resident((1, d)),
                  resident((d, 3 * d)), resident((d, d)),
                  resident((d, 2 * D_FF)), resident((D_FF, d)),
                  resident((1, d))],
        out_specs=tok,
        out_shape=jax.ShapeDtypeStruct((t, d), F32),
        compiler_params=_params("arbitrary"),
        name="tail",
    )(y_gla, y_ml, y_gdn, x2, mod, gain_mix, gain_ffn, w_merge, w_out, w_up, w_down, norm_final)


def _layout_w_in(w):
    gla_w = 2 * GLA_QK + 2 * D_MODEL
    ml_w = 2 * ML_QK + 2 * D_MODEL
    gdn_w = GDN_QKV + D_MODEL
    o_glr = gla_w
    o_ml = o_glr + GLA_GATE_RANK
    o_mlg = o_ml + ml_w
    o_gdn = o_mlg + 2 * ML_HEADS
    o_gdng = o_gdn + gdn_w
    o_merge = o_gdng + 2 * GDN_HEADS
    small = jnp.concatenate([w[:, o_glr:o_ml], w[:, o_mlg:o_gdn], w[:, o_gdng:o_merge]], axis=1)
    gate = jnp.pad(small, ((0, 0), (0, N_GATE - small.shape[1]))).astype(BF16)
    return (w[:, :gla_w].astype(BF16), w[:, o_ml:o_mlg].astype(BF16), w[:, o_gdn:o_gdng].astype(BF16),
            w[:, o_merge:].astype(BF16), gate)


def kernel(x, c, ada_w, ada_b, norm_mix, norm_ffn, w_in, gla_w_g2, gla_b_g2, gla_norm, ml_i_bias, ml_f_bias, ml_norm, gdn_conv, gdn_a_log, gdn_dt_bias, gdn_norm, w_out, w_ffn_up, w_ffn_down, norm_final):
    b, s, d = x.shape
    depth = ada_w.shape[0]
    mod = _adaln(c, ada_w, ada_b).reshape(depth, b, 6, d)
    x2 = x.reshape(b * s, d)
    fin = norm_final.reshape(1, d)
    for l in range(depth):
        w_gla, w_ml, w_gdn, w_merge, w_gate = _layout_w_in(w_in[l])
        gain = norm_mix[l].reshape(1, d)
        y_gla = _gla(x2, mod[l], gain, w_gla, w_gate, gla_w_g2[l], gla_b_g2[l], gla_norm[l], s)
        y_ml = _mlstm(x2, mod[l], gain, w_ml, w_gate, ml_i_bias[l], ml_f_bias[l], ml_norm[l], s)
        y_gdn = _gdn(x2, mod[l], gain, w_gdn, w_gate, gdn_conv[l], gdn_a_log[l], gdn_dt_bias[l],
                     gdn_norm[l], s)
        x2 = _tail(y_gla, y_ml, y_gdn, x2, mod[l], gain, norm_ffn[l].reshape(1, d), w_merge,
                   w_out[l].astype(BF16), w_ffn_up[l].astype(BF16), w_ffn_down[l].astype(BF16), fin, s,
                   final_norm=(l == depth - 1))
    return x2.reshape(b, s, d)
```

```python
import functools

import jax
import jax.numpy as jnp
from jax import lax
from jax.experimental import pallas as pl
from jax.experimental.pallas import tpu as pltpu

F32 = jnp.float32
BF16 = jnp.bfloat16

D_MODEL = 1024
NORM_EPS = 1e-6
GLA_HEADS, GLA_DK, GLA_DV = 4, 128, 256
GLA_QK = GLA_HEADS * GLA_DK
GLA_GATE_RANK = 16
GLA_GATE_NORM = 16.0
GLA_SUB = 16
ML_HEADS, ML_DK, ML_DV = 4, 128, 256
ML_QK = ML_HEADS * ML_DK
ML_GATE_CAP = 15.0
ML_M_INIT = -1e30
GDN_HEADS, GDN_DK, GDN_DV = 8, 128, 128
GDN_CONV = 4
GDN_QKV = GDN_HEADS * (2 * GDN_DK + GDN_DV)
D_FF = 2816
FF_CHUNK = 256

N_GATE = 128
LANE_GLR = 0
LANE_ML_I = 16
LANE_ML_F = 20
LANE_GDN_A = 24
LANE_GDN_B = 32

LANES = 128
SUBLANES = 8
VMEM_LIMIT = 56 * 1024 * 1024

MIX_BLOCK = 128
MIX_STEP = 256
TOK_TM = 512
CONV_COLS = 1024


def _dot(a, b):
    return jnp.dot(a, b, preferred_element_type=F32)


def _dot_nt(a, b):
    return lax.dot_general(a, b, (((1,), (1,)), ((), ())), preferred_element_type=F32)


def _dot_tn(a, b):
    return lax.dot_general(a, b, (((0,), (0,)), ((), ())), preferred_element_type=F32)


def _split3(x):
    hi = x.astype(BF16)
    r = x - hi.astype(F32)
    mid = r.astype(BF16)
    lo = (r - mid.astype(F32)).astype(BF16)
    return hi, mid, lo


def _dot_f32(a, b):
    a0, a1, a2 = _split3(a)
    b0, b1, b2 = _split3(b)
    small = _dot(a0, b2) + _dot(a2, b0) + _dot(a1, b1)
    return _dot(a0, b0) + (_dot(a0, b1) + _dot(a1, b0) + small)


def _cumsum_rows(tri, x):
    hi, mid, _ = _split3(x)
    return _dot(tri, hi) + _dot(tri, mid)


def _log_sigmoid(x):
    return jnp.minimum(x, 0.0) - jnp.log1p(jnp.exp(-jnp.abs(x)))


def _softplus(x):
    return jnp.maximum(x, 0.0) + jnp.log1p(jnp.exp(-jnp.abs(x)))


def _sigmoid(x):
    return 1.0 / (1.0 + jnp.exp(-x))


def _silu(x):
    return x * _sigmoid(x)


def _rms(x, gain):
    return x * lax.rsqrt(jnp.mean(x * x, axis=-1, keepdims=True) + NORM_EPS) * gain


def _modulated_norm(x, gain, shift, scale):
    return _rms(x, gain) * (1.0 + scale) + shift


def _params(*sem):
    return pltpu.CompilerParams(dimension_semantics=sem, vmem_limit_bytes=VMEM_LIMIT)


def _block_iotas(block):
    row = lax.broadcasted_iota(jnp.int32, (block, block), 0)
    col = lax.broadcasted_iota(jnp.int32, (block, block), 1)
    return row, col


def _adaln_kernel(c_ref, w_ref, b_ref, o_ref):
    c = c_ref[...]
    o_ref[...] = _dot_f32(_silu(c), w_ref[...]) + b_ref[...]


def _adaln(c, ada_w, ada_b):
    depth, d, n = ada_w.shape
    b = c.shape[0]
    tn = D_MODEL
    return pl.pallas_call(
        _adaln_kernel,
        grid=(depth, n // tn),
        in_specs=[
            pl.BlockSpec((b, d), lambda l, j: (0, 0)),
            pl.BlockSpec((None, d, tn), lambda l, j: (l, 0, j)),
            pl.BlockSpec((None, 1, tn), lambda l, j: (l, 0, j)),
        ],
        out_specs=pl.BlockSpec((None, b, tn), lambda l, j: (l, 0, j)),
        out_shape=jax.ShapeDtypeStruct((depth, b, n), F32),
        compiler_params=_params("arbitrary", "arbitrary"),
        name="adaln",
    )(c, ada_w, ada_b.reshape(depth, 1, n))


def _zero_state(state):
    for ref in state:
        ref[...] = jnp.zeros_like(ref)


def _mixer_kernel(x_ref, mod_ref, gain_ref, w_ref, wg_ref, *rest,
                  body, gates, reset, pre, n_params, row0, steps_per_seq):
    params = rest[:n_params]
    o_ref = rest[n_params]
    p_refs = rest[n_params + 1:n_params + 3]
    g_refs = rest[n_params + 3:n_params + 5]
    state = rest[n_params + 5:]
    s = pl.program_id(0)
    step = x_ref.shape[0]

    @pl.when(s == 0)
    def _():
        for ref in (*p_refs, *g_refs, *state):
            ref[...] = jnp.zeros_like(ref)

    t = (s + steps_per_seq - 1) % steps_per_seq

    @pl.when(t == 0)
    def _():
        reset(state)

    for parity in (0, 1):
        @pl.when(s % 2 == parity)
        def _(parity=parity):
            p_new, p_cur = p_refs[parity], p_refs[1 - parity]
            if pre is not None:
                pre(p_cur, p_new, t)
            gate_vals = gates(g_refs[1 - parity], params, p_cur, state)
            h = _modulated_norm(x_ref[...], gain_ref[...], mod_ref[0:1, :], mod_ref[1:2, :]).astype(BF16)
            p_new[row0:row0 + step, :] = _dot(h, w_ref[...])
            g_refs[parity][...] = _dot(h, wg_ref[...])
            body(p_cur, gate_vals, params, o_ref, state)


def _mixer_call(name, gates, body, x2, mod, gain, w, w_gate, params, param_specs, state_shapes,
                seq, reset=_zero_state, pre=None, row0=0):
    t, d = x2.shape
    step = min(MIX_STEP, seq)
    n = t // step
    per_seq = seq // step
    n_cols = w.shape[1]

    def cur(s):
        return jnp.minimum(s, n - 1)

    return pl.pallas_call(
        functools.partial(_mixer_kernel, body=body, gates=gates, reset=reset, pre=pre,
                          n_params=len(params), row0=row0, steps_per_seq=per_seq),
        grid=(n + 1,),
        in_specs=[
            pl.BlockSpec((step, d), lambda s: (cur(s), 0)),
            pl.BlockSpec((None, 6, d), lambda s: (cur(s) // per_seq, 0, 0)),
            pl.BlockSpec((1, d), lambda s: (0, 0)),
            pl.BlockSpec((d, n_cols), lambda s: (0, 0)),
            pl.BlockSpec((d, N_GATE), lambda s: (0, 0)),
            *param_specs,
        ],
        out_specs=pl.BlockSpec((step, d), lambda s: (jnp.maximum(s - 1, 0), 0)),
        out_shape=jax.ShapeDtypeStruct((t, d), BF16),
        scratch_shapes=[
            pltpu.VMEM((row0 + step, n_cols), F32),
            pltpu.VMEM((row0 + step, n_cols), F32),
            pltpu.VMEM((step, N_GATE), F32),
            pltpu.VMEM((step, N_GATE), F32),
            *state_shapes,
        ],
        compiler_params=_params("arbitrary"),
        name=name,
    )(x2, mod, gain, w, w_gate, *params)


def _full_spec(shape):
    return pl.BlockSpec(shape, lambda s: (0,) * len(shape))


def _row_refs(c, block, seg, pick):
    parts = []
    for start in range(0, block, seg):
        r = pick(start)
        row = jnp.zeros((1, c.shape[1]), F32) if r is None else c[r:r + 1, :]
        parts.append(jnp.broadcast_to(row, (seg, c.shape[1])))
    return parts[0] if len(parts) == 1 else jnp.concatenate(parts, axis=0)


def _gla_gates(g_ref, params, *_):
    wg_ref, bg_ref, _ = params
    step = g_ref.shape[0]
    block = min(MIX_BLOCK, step)
    row, col = _block_iotas(block)
    tri = (col <= row).astype(BF16)
    w0, w1, _ = _split3(wg_ref[...])
    cum = []
    for b in range(step // block):
        g0, g1, _ = _split3(g_ref[b * block:(b + 1) * block, :])
        x = _dot(g0, w0) + (_dot(g0, w1) + _dot(g1, w0)) + bg_ref[...]
        cum.append(_cumsum_rows(tri, _log_sigmoid(x) * (1.0 / GLA_GATE_NORM)))
    return cum


def _gla_body(p_ref, cum, params, o_ref, state):
    _, _, gain_ref = params
    (st_ref,) = state
    step = o_ref.shape[0]
    block = min(MIX_BLOCK, step)
    off_k, off_v, off_r = GLA_QK, 2 * GLA_QK, 2 * GLA_QK + D_MODEL

    row, col = _block_iotas(block)
    causal = col <= row
    sub_xor = (row // GLA_SUB) ^ (col // GLA_SUB)
    n_levels = (block // GLA_SUB).bit_length() - 1
    gain = gain_ref[...]
    blocks = range(step // block)
    units = [(b, h) for b in blocks for h in range(GLA_HEADS)]

    def rows(b, c0, width):
        return p_ref[b * block:(b + 1) * block, c0:c0 + width]

    c = {u: cum[u[0]][:, u[1] * GLA_DK:(u[1] + 1) * GLA_DK] for u in units}
    q = {u: rows(u[0], u[1] * GLA_DK, GLA_DK) * (GLA_DK ** -0.5) for u in units}
    k = {u: rows(u[0], off_k + u[1] * GLA_DK, GLA_DK) for u in units}
    v = {u: rows(u[0], off_v + u[1] * GLA_DV, GLA_DV).astype(BF16) for u in units}

    ref0 = {u: _row_refs(c[u], block, GLA_SUB, lambda s: None if s == 0 else s - 1) for u in units}
    p = {u: _dot_nt((q[u] * jnp.exp(c[u] - ref0[u])).astype(BF16),
                    (k[u] * jnp.exp(ref0[u] - c[u])).astype(BF16)) for u in units}
    a = {u: jnp.where(causal & (sub_xor == 0), p[u], 0.0) for u in units}
    for lev in range(1, n_levels + 1):
        half = GLA_SUB << (lev - 1)
        sel = causal & (sub_xor >= (1 << (lev - 1))) & (sub_xor < (1 << lev))
        ref = {u: _row_refs(c[u], block, 2 * half, lambda s: s + half - 1) for u in units}
        qs = {u: (q[u] * jnp.exp(jnp.minimum(c[u] - ref[u], 0.0))).astype(BF16) for u in units}
        ks = {u: (k[u] * jnp.exp(jnp.minimum(ref[u] - c[u], 0.0))).astype(BF16) for u in units}
        p = {u: _dot_nt(qs[u], ks[u]) for u in units}
        a = {u: jnp.where(sel, p[u], a[u]) for u in units}
    o_intra = {u: _dot(a[u].astype(BF16), v[u]) for u in units}
    q_dec = {u: (q[u] * jnp.exp(c[u])).astype(BF16) for u in units}
    c_end = {u: c[u][block - 1:block, :] for u in units}
    k_end = {u: (k[u] * jnp.exp(c_end[u] - c[u])).astype(BF16) for u in units}
    kv = {u: _dot_tn(v[u], k_end[u]) for u in units}

    st = [st_ref[h] for h in range(GLA_HEADS)]
    for b in blocks:
        heads = [(b, h) for h in range(GLA_HEADS)]
        o_inter = [_dot_nt(q_dec[u], st[u[1]].astype(BF16)) for u in heads]
        st = [st[u[1]] * jnp.exp(c_end[u]) + kv[u] for u in heads]
        for u, oi in zip(heads, o_inter):
            r = rows(b, off_r + u[1] * GLA_DV, GLA_DV)
            y = _rms(o_intra[u] + oi, gain) * _silu(r)
            o_ref[b * block:(b + 1) * block, u[1] * GLA_DV:(u[1] + 1) * GLA_DV] = y.astype(o_ref.dtype)
    for h in range(GLA_HEADS):
        st_ref[h] = st[h]


def _gla(x2, mod, gain, w, w_gate, w_g2, b_g2, norm_gain, seq):
    w_g2p = jnp.zeros((N_GATE, GLA_QK), F32).at[LANE_GLR:LANE_GLR + GLA_GATE_RANK].set(w_g2)
    params = (w_g2p, b_g2.reshape(1, GLA_QK), norm_gain.reshape(1, GLA_DV))
    specs = [_full_spec(p.shape) for p in params]
    state = [pltpu.VMEM((GLA_HEADS, GLA_DV, GLA_DK), F32)]
    return _mixer_call("gla", _gla_gates, _gla_body, x2, mod, gain, w, w_gate, params, specs, state,
                       seq)


def _mlstm_gates(g_ref, params, *_):
    bias_ref, _ = params
    step = g_ref.shape[0]
    block = min(MIX_BLOCK, step)
    row, col = _block_iotas(block)
    tri = (col <= row).astype(BF16)
    log_i, log_i_t, cum, cum_t = [], [], [], []
    for b in range(step // block):
        capped = ML_GATE_CAP * jnp.tanh((g_ref[b * block:(b + 1) * block, :] + bias_ref[...])
                                        * (1.0 / ML_GATE_CAP))
        cs_b = _cumsum_rows(tri, _log_sigmoid(capped))
        log_i.append(capped)
        log_i_t.append(capped.T)
        cum.append(cs_b)
        cum_t.append(cs_b.T)
    return log_i, log_i_t, cum, cum_t


def _mlstm_body(p_ref, gate_vals, params, o_ref, state):
    _, gain_ref = params
    cs_ref, ns_ref, m_ref = state
    log_i, log_i_t, cum, cum_t = gate_vals
    step = o_ref.shape[0]
    block = min(MIX_BLOCK, step)
    off_k, off_v, off_o = ML_QK, 2 * ML_QK, 2 * ML_QK + D_MODEL

    row, col = _block_iotas(block)
    causal = col <= row
    gain = gain_ref[...]
    blocks = range(step // block)
    heads = range(ML_HEADS)
    units = [(b, h) for b in blocks for h in heads]

    def rows(b, c0, width):
        return p_ref[b * block:(b + 1) * block, c0:c0 + width]

    li_c = {u: log_i[u[0]][:, LANE_ML_I + u[1]:LANE_ML_I + u[1] + 1] for u in units}
    li_r = {u: log_i_t[u[0]][LANE_ML_I + u[1]:LANE_ML_I + u[1] + 1, :] for u in units}
    cum_c = {u: cum[u[0]][:, LANE_ML_F + u[1]:LANE_ML_F + u[1] + 1] for u in units}
    cum_r = {u: cum_t[u[0]][LANE_ML_F + u[1]:LANE_ML_F + u[1] + 1, :] for u in units}
    cum_last = {u: cum_c[u][block - 1:block, :] for u in units}
    log_end = {u: cum_last[u] - cum_c[u] + li_c[u] for u in units}
    m_prev, m_new = {}, {}
    for h in heads:
        m = m_ref[h:h + 1, 0:1]
        for b in blocks:
            u = (b, h)
            m_prev[u] = m
            m = jnp.maximum(cum_last[u] + m, jnp.max(log_end[u], axis=0, keepdims=True))
            m_new[u] = m
        m_ref[h:h + 1, :] = jnp.broadcast_to(m, (1, LANES))

    q = {u: rows(u[0], u[1] * ML_DK, ML_DK) * (ML_DK ** -0.5) for u in units}
    qb = {u: q[u].astype(BF16) for u in units}
    k = {u: rows(u[0], off_k + u[1] * ML_DK, ML_DK) for u in units}
    v = {u: rows(u[0], off_v + u[1] * ML_DV, ML_DV).astype(BF16) for u in units}
    qk = {u: _dot_nt(qb[u], k[u].astype(BF16)) for u in units}
    log_d = {u: jnp.where(causal, cum_c[u] - cum_r[u] + li_r[u], -jnp.inf) for u in units}
    m_inter = {u: cum_c[u] + m_prev[u] for u in units}
    m_t = {u: jnp.maximum(m_inter[u], jnp.max(log_d[u], axis=1, keepdims=True)) for u in units}
    a_inter = {u: jnp.exp(m_inter[u] - m_t[u]) for u in units}
    p = {u: qk[u] * jnp.exp(log_d[u] - m_t[u]) for u in units}
    pv = {u: _dot(p[u].astype(BF16), v[u]) for u in units}
    p_sum = {u: jnp.sum(p[u], axis=1, keepdims=True) for u in units}
    dec = {u: jnp.exp(cum_last[u] + m_prev[u] - m_new[u]) for u in units}
    kw = {u: k[u] * jnp.exp(log_end[u] - m_new[u]) for u in units}
    kwv = {u: _dot_tn(kw[u].astype(BF16), v[u]) for u in units}
    kw_sum = {u: jnp.sum(kw[u], axis=0, keepdims=True) for u in units}

    cs = [cs_ref[h] for h in heads]
    ns = [ns_ref[h:h + 1, :] for h in heads]
    for b in blocks:
        q_cs = [_dot(qb[(b, h)], cs[h].astype(BF16)) for h in heads]
        q_ns = [jnp.sum(q[(b, h)] * ns[h], axis=1, keepdims=True) for h in heads]
        cs = [dec[(b, h)] * cs[h] + kwv[(b, h)] for h in heads]
        ns = [dec[(b, h)] * ns[h] + kw_sum[(b, h)] for h in heads]
        for h in heads:
            u = (b, h)
            num = pv[u] + a_inter[u] * q_cs[h]
            den = p_sum[u] + a_inter[u] * q_ns[h]
            hid = num / jnp.maximum(jnp.abs(den), jnp.exp(-m_t[u]))
            og = rows(b, off_o + h * ML_DV, ML_DV)
            y = _rms(hid, gain) * _sigmoid(og)
            o_ref[b * block:(b + 1) * block, h * ML_DV:(h + 1) * ML_DV] = y.astype(o_ref.dtype)
    for h in heads:
        cs_ref[h] = cs[h]
        ns_ref[h:h + 1, :] = ns[h]


def _mlstm_reset(state):
    cs_ref, ns_ref, m_ref = state
    cs_ref[...] = jnp.zeros_like(cs_ref)
    ns_ref[...] = jnp.zeros_like(ns_ref)
    m_ref[...] = jnp.full_like(m_ref, ML_M_INIT)


def _mlstm(x2, mod, gain, w, w_gate, i_bias, f_bias, norm_gain, seq):
    bias = jnp.zeros((1, N_GATE), F32)
    bias = bias.at[0, LANE_ML_I:LANE_ML_I + ML_HEADS].set(i_bias)
    bias = bias.at[0, LANE_ML_F:LANE_ML_F + ML_HEADS].set(f_bias)
    params = (bias, norm_gain.reshape(1, ML_DV))
    specs = [_full_spec(p.shape) for p in params]
    state = [
        pltpu.VMEM((ML_HEADS, ML_DK, ML_DV), F32),
        pltpu.VMEM((SUBLANES, ML_DK), F32),
        pltpu.VMEM((SUBLANES, LANES), F32),
    ]
    return _mixer_call("mlstm", _mlstm_gates, _mlstm_body, x2, mod, gain, w, w_gate, params, specs,
                       state, seq, reset=_mlstm_reset)


def _gdn_pre(p_cur, p_new, t):
    step = p_cur.shape[0] - SUBLANES
    tail = p_new[step:step + SUBLANES, 0:GDN_QKV]
    p_cur[0:SUBLANES, 0:GDN_QKV] = jnp.where(t == 0, 0.0, tail)


def _gdn_conv(p_ref, conv_ref, qkv_ref, block):
    ext = block + 2 * SUBLANES
    n_delayed = GDN_CONV - 1
    r = lax.broadcasted_iota(jnp.int32, (block, n_delayed * ext), 0)
    c = lax.broadcasted_iota(jnp.int32, (block, n_delayed * ext), 1)
    shift = None
    for j in range(n_delayed):
        hit = c == j * ext + SUBLANES + r - (n_delayed - j)
        shift = hit if shift is None else shift | hit
    shift = shift.astype(BF16)
    for b in range(qkv_ref.shape[0] // block):
        for c0 in range(0, GDN_QKV, CONV_COLS):
            xe = p_ref[b * block:b * block + SUBLANES + block, c0:c0 + CONV_COLS]
            pad = jnp.zeros((SUBLANES, CONV_COLS), F32)
            parts = []
            for j in range(n_delayed):
                parts += [xe * conv_ref[j:j + 1, c0:c0 + CONV_COLS], pad]
            delayed = _dot(shift, jnp.concatenate(parts, axis=0).astype(BF16))
            y = delayed + xe[SUBLANES:, :] * conv_ref[n_delayed:GDN_CONV, c0:c0 + CONV_COLS]
            qkv_ref[b * block:(b + 1) * block, c0:c0 + CONV_COLS] = _silu(y)


def _gdn_gates(g_ref, params, p_ref, state):
    conv_ref, prm_ref, _ = params
    step = g_ref.shape[0]
    block = min(MIX_BLOCK, step)
    _gdn_conv(p_ref, conv_ref, state[1], block)
    row, col = _block_iotas(block)
    tri = (col <= row).astype(BF16)
    cum, cum_t, beta = [], [], []
    for b in range(step // block):
        g = g_ref[b * block:(b + 1) * block, :]
        decay = -jnp.exp(prm_ref[0:1, :]) * _softplus(g + prm_ref[1:2, :])
        cs_b = _cumsum_rows(tri, decay)
        cum.append(cs_b)
        cum_t.append(cs_b.T)
        beta.append(_sigmoid(g))
    return cum, cum_t, beta


def _gdn_body(p_ref, gate_vals, params, o_ref, state):
    _, _, gain_ref = params
    s_ref, qkv_ref = state
    cum, cum_t, beta = gate_vals
    step = o_ref.shape[0]
    block = min(MIX_BLOCK, step)
    off_z = GDN_QKV

    def conv_silu(b, c0):
        return qkv_ref[b * block:(b + 1) * block, c0:c0 + LANES]

    row, col = _block_iotas(block)
    causal = col <= row
    strict = col < row
    eye = (col == row).astype(F32)
    level_masks = [
        ((row >> j == col >> j) & (row >> (j - 1) != col >> (j - 1))).astype(F32)
        for j in range(1, block.bit_length())
    ]
    gain = gain_ref[...]
    blocks = range(step // block)
    heads = range(GDN_HEADS)
    units = [(b, h) for b in blocks for h in heads]

    cum_c = {u: cum[u[0]][:, LANE_GDN_A + u[1]:LANE_GDN_A + u[1] + 1] for u in units}
    cum_r = {u: cum_t[u[0]][LANE_GDN_A + u[1]:LANE_GDN_A + u[1] + 1, :] for u in units}
    beta_c = {u: beta[u[0]][:, LANE_GDN_B + u[1]:LANE_GDN_B + u[1] + 1] for u in units}
    q = {u: conv_silu(u[0], u[1] * GDN_DK) for u in units}
    k = {u: conv_silu(u[0], GDN_HEADS * GDN_DK + u[1] * GDN_DK) for u in units}
    v = {u: conv_silu(u[0], 2 * GDN_HEADS * GDN_DK + u[1] * GDN_DV) for u in units}
    q = {u: q[u] * lax.rsqrt(jnp.sum(q[u] * q[u], axis=-1, keepdims=True) + NORM_EPS) * (GDN_DK ** -0.5)
         for u in units}
    k = {u: k[u] * lax.rsqrt(jnp.sum(k[u] * k[u], axis=-1, keepdims=True) + NORM_EPS) for u in units}
    qb = {u: q[u].astype(BF16) for u in units}
    kb = {u: k[u].astype(BF16) for u in units}
    gamma = {u: jnp.where(causal, jnp.exp(jnp.minimum(cum_c[u] - cum_r[u], 0.0)), 0.0) for u in units}
    kk = {u: _dot_nt(kb[u], kb[u]) for u in units}
    a = {u: jnp.where(strict, beta_c[u] * kk[u] * gamma[u], 0.0) for u in units}
    inv = {u: eye - a[u] * level_masks[0] for u in units}
    for mask in level_masks[1:]:
        xb = {u: inv[u].astype(BF16) for u in units}
        xa = {u: _dot(xb[u], (a[u] * mask).astype(BF16)) for u in units}
        xax = {u: _dot(xa[u].astype(BF16), xb[u]) for u in units}
        inv = {u: inv[u] - xax[u] for u in units}

    e_cum = {u: jnp.exp(cum_c[u]) for u in units}
    rhs = {u: jnp.concatenate([v[u] * beta_c[u], k[u] * (beta_c[u] * e_cum[u])], axis=1).astype(BF16)
           for u in units}
    uw = {u: _dot(inv[u].astype(BF16), rhs[u]) for u in units}
    a_qk = {u: (_dot_nt(qb[u], kb[u]) * gamma[u]).astype(BF16) for u in units}
    q_dec = {u: (q[u] * e_cum[u]).astype(BF16) for u in units}
    cum_last = {u: cum_c[u][block - 1:block, :] for u in units}
    k_end = {u: (k[u] * jnp.exp(cum_last[u] - cum_c[u])).astype(BF16) for u in units}

    st = [s_ref[h] for h in heads]
    for b in blocks:
        us = [(b, h) for h in heads]
        sb = [t_.astype(BF16) for t_ in st]
        ws = [_dot(uw[u][:, GDN_DV:].astype(BF16), sb[u[1]]) for u in us]
        qs = [_dot(q_dec[u], sb[u[1]]) for u in us]
        vb = [(uw[u][:, :GDN_DV] - ws[u[1]]).astype(BF16) for u in us]
        o = [qs[u[1]] + _dot(a_qk[u], vb[u[1]]) for u in us]
        st = [st[u[1]] * jnp.exp(cum_last[u]) + _dot_tn(k_end[u], vb[u[1]]) for u in us]
        for u in us:
            h = u[1]
            z = p_ref[SUBLANES + b * block:SUBLANES + (b + 1) * block, off_z + h * GDN_DV:off_z + (h + 1) * GDN_DV]
            y = _rms(o[h], gain) * _silu(z)
            o_ref[b * block:(b + 1) * block, h * GDN_DV:(h + 1) * GDN_DV] = y.astype(o_ref.dtype)
    for h in heads:
        s_ref[h] = st[h]


def _gdn_reset(state):
    s_ref, _ = state
    s_ref[...] = jnp.zeros_like(s_ref)


def _gdn(x2, mod, gain, w, w_gate, conv_w, a_log, dt_bias, norm_gain, seq):
    prm = jnp.zeros((SUBLANES, N_GATE), F32)
    prm = prm.at[0, LANE_GDN_A:LANE_GDN_A + GDN_HEADS].set(a_log)
    prm = prm.at[1, LANE_GDN_A:LANE_GDN_A + GDN_HEADS].set(dt_bias)
    params = (conv_w, prm, norm_gain.reshape(1, GDN_DV))
    specs = [_full_spec(p.shape) for p in params]
    step = min(MIX_STEP, seq)
    state = [pltpu.VMEM((GDN_HEADS, GDN_DK, GDN_DV), F32), pltpu.VMEM((step, GDN_QKV), F32)]
    return _mixer_call("gdn", _gdn_gates, _gdn_body, x2, mod, gain, w, w_gate, params, specs, state,
                       seq, reset=_gdn_reset, pre=_gdn_pre, row0=SUBLANES)


def _tail_kernel(yg_ref, ym_ref, yd_ref, x_ref, mod_ref, gmix_ref, gffn_ref, wm_ref, wo_ref, wu_ref,
                 wd_ref, fin_ref, o_ref, *, final_norm):
    x = x_ref[...]
    d = x.shape[1]
    h = _modulated_norm(x, gmix_ref[...], mod_ref[0:1, :], mod_ref[1:2, :]).astype(BF16)
    y = None
    for i, y_ref in enumerate((yg_ref, ym_ref, yd_ref)):
        term = _sigmoid(_dot(h, wm_ref[:, i * d:(i + 1) * d])) * y_ref[...].astype(F32)
        y = term if y is None else y + term
    x = x + mod_ref[2:3, :] * _dot(y.astype(BF16), wo_ref[...])

    h = _modulated_norm(x, gffn_ref[...], mod_ref[3:4, :], mod_ref[4:5, :]).astype(BF16)
    acc = None
    for c in range(D_FF // FF_CHUNK):
        gate = _dot(h, wu_ref[:, c * FF_CHUNK:(c + 1) * FF_CHUNK])
        val = _dot(h, wu_ref[:, D_FF + c * FF_CHUNK:D_FF + (c + 1) * FF_CHUNK])
        part = _dot((_silu(gate) * val).astype(BF16), wd_ref[c * FF_CHUNK:(c + 1) * FF_CHUNK, :])
        acc = part if acc is None else acc + part
    y = x + mod_ref[5:6, :] * acc
    if final_norm:
        y = _rms(y, fin_ref[...])
    o_ref[...] = y


def _tail(y_gla, y_ml, y_gdn, x2, mod, gain_mix, gain_ffn, w_merge, w_out, w_up, w_down, norm_final,
          seq, final_norm):
    t, d = x2.shape
    tm = min(TOK_TM, seq)
    per_seq = seq // tm
    tok = pl.BlockSpec((tm, d), lambda i: (i, 0))

    def resident(shape):
        return pl.BlockSpec(shape, lambda i: (0, 0), pipeline_mode=pl.Buffered(1))

    return pl.pallas_call(
        functools.partial(_tail_kernel, final_norm=final_norm),
        grid=(t // tm,),
        in_specs=[tok, tok, tok, tok,
                  pl.BlockSpec((None, 6, d), lambda i: (i // per_seq, 0, 0)),
                  resident((1, d)), resident((1, d)),
                  resident((d, 3 * d)), resident((d, d)),
                  resident((d, 2 * D_FF)), resident((D_FF, d)),
                  resident((1, d))],
        out_specs=tok,
        out_shape=jax.ShapeDtypeStruct((t, d), F32),
        compiler_params=_params("arbitrary"),
        name="tail",
    )(y_gla, y_ml, y_gdn, x2, mod, gain_mix, gain_ffn, w_merge, w_out, w_up, w_down, norm_final)


def _layout_w_in(w):
    gla_w = 2 * GLA_QK + 2 * D_MODEL
    ml_w = 2 * ML_QK + 2 * D_MODEL
    gdn_w = GDN_QKV + D_MODEL
    o_glr = gla_w
    o_ml = o_glr + GLA_GATE_RANK
    o_mlg = o_ml + ml_w
    o_gdn = o_mlg + 2 * ML_HEADS
    o_gdng = o_gdn + gdn_w
    o_merge = o_gdng + 2 * GDN_HEADS
    small = jnp.concatenate([w[:, o_glr:o_ml], w[:, o_mlg:o_gdn], w[:, o_gdng:o_merge]], axis=1)
    gate = jnp.pad(small, ((0, 0), (0, N_GATE - small.shape[1]))).astype(BF16)
    return (w[:, :gla_w].astype(BF16), w[:, o_ml:o_mlg].astype(BF16), w[:, o_gdn:o_gdng].astype(BF16),
            w[:, o_merge:].astype(BF16), gate)


def kernel(x, c, ada_w, ada_b, norm_mix, norm_ffn, w_in, gla_w_g2, gla_b_g2, gla_norm, ml_i_bias, ml_f_bias, ml_norm, gdn_conv, gdn_a_log, gdn_dt_bias, gdn_norm, w_out, w_ffn_up, w_ffn_down, norm_final):
    b, s, d = x.shape
    depth = ada_w.shape[0]
    mod = _adaln(c, ada_w, ada_b).reshape(depth, b, 6, d)
    x2 = x.reshape(b * s, d)
    fin = norm_final.reshape(1, d)
    for l in range(depth):
        w_gla, w_ml, w_gdn, w_merge, w_gate = _layout_w_in(w_in[l])
        gain = norm_mix[l].reshape(1, d)
        y_gla = _gla(x2, mod[l], gain, w_gla, w_gate, gla_w_g2[l], gla_b_g2[l], gla_norm[l], s)
        y_ml = _mlstm(x2, mod[l], gain, w_ml, w_gate, ml_i_bias[l], ml_f_bias[l], ml_norm[l], s)
        y_gdn = _gdn(x2, mod[l], gain, w_gdn, w_gate, gdn_conv[l], gdn_a_log[l], gdn_dt_bias[l],
                     gdn_norm[l], s)
        x2 = _tail(y_gla, y_ml, y_gdn, x2, mod[l], gain, norm_ffn[l].reshape(1, d), w_merge,
                   w_out[l].astype(BF16), w_ffn_up[l].astype(BF16), w_ffn_down[l].astype(BF16), fin, s,
                   final_norm=(l == depth - 1))
    return x2.reshape(b, s, d)
```

```python
import functools
from typing import Callable, NamedTuple

import jax
import jax.numpy as jnp
from jax import lax
from jax.experimental import pallas as pl
from jax.experimental.pallas import tpu as pltpu

F32 = jnp.float32
BF16 = jnp.bfloat16

D_MODEL = 1024
NORM_EPS = 1e-6
GLA_HEADS, GLA_DK, GLA_DV = 4, 128, 256
GLA_QK = GLA_HEADS * GLA_DK
GLA_GATE_RANK = 16
GLA_GATE_NORM = 16.0
GLA_SUB = 16
ML_HEADS, ML_DK, ML_DV = 4, 128, 256
ML_QK = ML_HEADS * ML_DK
ML_GATE_CAP = 15.0
ML_M_INIT = -1e30
GDN_HEADS, GDN_DK, GDN_DV = 8, 128, 128
GDN_CONV = 4
GDN_QKV = GDN_HEADS * (2 * GDN_DK + GDN_DV)
D_FF = 2816
FF_CHUNK = 256

N_GATE = 128
LANE_GLR = 0
LANE_ML_I = 16
LANE_ML_F = 20
LANE_GDN_A = 24
LANE_GDN_B = 32

LANES = 128
SUBLANES = 8
VMEM_LIMIT = 56 * 1024 * 1024

MIX_BLOCK = 128
MIX_STEP = 512
GDN_STEP = 256
TOK_TM = 512
CONV_COLS = 1024


def _dot(a, b):
    return jnp.dot(a, b, preferred_element_type=F32)


def _dot_nt(a, b):
    return lax.dot_general(a, b, (((1,), (1,)), ((), ())), preferred_element_type=F32)


def _dot_tn(a, b):
    return lax.dot_general(a, b, (((0,), (0,)), ((), ())), preferred_element_type=F32)


def _split3(x):
    hi = x.astype(BF16)
    r = x - hi.astype(F32)
    mid = r.astype(BF16)
    lo = (r - mid.astype(F32)).astype(BF16)
    return hi, mid, lo


def _dot_f32(a, b):
    a0, a1, _ = _split3(a)
    b0, b1, _ = _split3(b)
    return _dot(a0, b0) + (_dot(a0, b1) + _dot(a1, b0))


def _cumsum_rows(tri, x):
    hi, mid, _ = _split3(x)
    return _dot(tri, hi) + _dot(tri, mid)


def _log_sigmoid(x):
    return jnp.minimum(x, 0.0) - jnp.log1p(jnp.exp(-jnp.abs(x)))


def _softplus(x):
    return jnp.maximum(x, 0.0) + jnp.log1p(jnp.exp(-jnp.abs(x)))


def _sigmoid(x):
    return 1.0 / (1.0 + jnp.exp(-x))


def _silu(x):
    return x * _sigmoid(x)


def _rms(x, gain):
    return x * lax.rsqrt(jnp.mean(x * x, axis=-1, keepdims=True) + NORM_EPS) * gain


def _modulated_norm(x, gain, shift, scale):
    return _rms(x, gain) * (1.0 + scale) + shift


def _params(*sem):
    return pltpu.CompilerParams(dimension_semantics=sem, vmem_limit_bytes=VMEM_LIMIT)


def _block_iotas(block):
    row = lax.broadcasted_iota(jnp.int32, (block, block), 0)
    col = lax.broadcasted_iota(jnp.int32, (block, block), 1)
    return row, col


def _adaln_kernel(c_ref, w_ref, b_ref, o_ref):
    c = c_ref[...]
    o_ref[...] = _dot_f32(_silu(c), w_ref[...]) + b_ref[...]


def _adaln(c, ada_w, ada_b):
    depth, d, n = ada_w.shape
    b = c.shape[0]
    tn = D_MODEL
    return pl.pallas_call(
        _adaln_kernel,
        grid=(depth, n // tn),
        in_specs=[
            pl.BlockSpec((b, d), lambda l, j: (0, 0)),
            pl.BlockSpec((None, d, tn), lambda l, j: (l, 0, j)),
            pl.BlockSpec((None, 1, tn), lambda l, j: (l, 0, j)),
        ],
        out_specs=pl.BlockSpec((None, b, tn), lambda l, j: (l, 0, j)),
        out_shape=jax.ShapeDtypeStruct((depth, b, n), F32),
        compiler_params=_params("arbitrary", "arbitrary"),
        name="adaln",
    )(c, ada_w, ada_b.reshape(depth, 1, n))


def _zero_state(state):
    for ref in state:
        ref[...] = jnp.zeros_like(ref)


class _Mixer(NamedTuple):
    gates: Callable
    body: Callable
    w: jax.Array
    params: tuple
    state_shapes: tuple
    reset: Callable = _zero_state
    pre: Callable | None = None
    row0: int = 0


def _mixer_kernel(x_ref, mod_ref, gain_ref, wg_ref, *rest, mixers, steps_per_seq):
    n = len(mixers)
    pos = 0
    w_refs, params = [], []
    for m in mixers:
        w_refs.append(rest[pos])
        params.append(rest[pos + 1:pos + 1 + len(m.params)])
        pos += 1 + len(m.params)
    o_refs = rest[pos:pos + n]
    g_refs = rest[pos + n:pos + n + 2]
    pos += n + 2
    p_refs, states = [], []
    for m in mixers:
        p_refs.append(rest[pos:pos + 2])
        states.append(rest[pos + 2:pos + 2 + len(m.state_shapes)])
        pos += 2 + len(m.state_shapes)
    s = pl.program_id(0)
    step = x_ref.shape[0]

    @pl.when(s == 0)
    def _():
        for ref in (*g_refs, *[r for pr in p_refs for r in pr], *[r for st in states for r in st]):
            ref[...] = jnp.zeros_like(ref)

    t = (s + steps_per_seq - 1) % steps_per_seq

    @pl.when(t == 0)
    def _():
        for m, st in zip(mixers, states):
            m.reset(st)

    for parity in (0, 1):
        @pl.when(s % 2 == parity)
        def _(parity=parity):
            gate_vals = []
            for i, m in enumerate(mixers):
                p_new, p_cur = p_refs[i][parity], p_refs[i][1 - parity]
                if m.pre is not None:
                    m.pre(p_cur, p_new, t)
                gate_vals.append(m.gates(g_refs[1 - parity], params[i], p_cur, states[i]))
            h = _modulated_norm(x_ref[...], gain_ref[...], mod_ref[0:1, :], mod_ref[1:2, :]).astype(BF16)
            for i, m in enumerate(mixers):
                p_refs[i][parity][m.row0:m.row0 + step, :] = _dot(h, w_refs[i][...])
            g_refs[parity][...] = _dot(h, wg_ref[...])
            for i, m in enumerate(mixers):
                m.body(p_refs[i][1 - parity], gate_vals[i], params[i], o_refs[i], states[i])


def _mixer_call(name, mixers, x2, mod, gain, w_gate, seq, step):
    t, d = x2.shape
    step = min(step, seq)
    n = t // step
    per_seq = seq // step

    def cur(s):
        return jnp.minimum(s, n - 1)

    operands, specs, scratch = [], [], [pltpu.VMEM((step, N_GATE), F32)] * 2
    for m in mixers:
        operands += [m.w, *m.params]
        specs += [pl.BlockSpec(m.w.shape, lambda s: (0, 0), pipeline_mode=pl.Buffered(1)),
                  *[_full_spec(p.shape) for p in m.params]]
        scratch += [pltpu.VMEM((m.row0 + step, m.w.shape[1]), F32)] * 2 + list(m.state_shapes)
    out_spec = pl.BlockSpec((step, d), lambda s: (jnp.maximum(s - 1, 0), 0))
    return pl.pallas_call(
        functools.partial(_mixer_kernel, steps_per_seq=per_seq,
                          mixers=tuple(m._replace(w=None, params=(None,) * len(m.params)) for m in mixers)),
        grid=(n + 1,),
        in_specs=[
            pl.BlockSpec((step, d), lambda s: (cur(s), 0)),
            pl.BlockSpec((None, 6, d), lambda s: (cur(s) // per_seq, 0, 0)),
            pl.BlockSpec((1, d), lambda s: (0, 0)),
            pl.BlockSpec((d, N_GATE), lambda s: (0, 0)),
            *specs,
        ],
        out_specs=[out_spec] * len(mixers),
        out_shape=[jax.ShapeDtypeStruct((t, d), BF16)] * len(mixers),
        scratch_shapes=scratch,
        compiler_params=_params("arbitrary"),
        name=name,
    )(x2, mod, gain, w_gate, *operands)


def _full_spec(shape):
    return pl.BlockSpec(shape, lambda s: (0,) * len(shape))


def _row_refs(c, block, seg, pick):
    parts = []
    for start in range(0, block, seg):
        r = pick(start)
        row = jnp.zeros((1, c.shape[1]), F32) if r is None else c[r:r + 1, :]
        parts.append(jnp.broadcast_to(row, (seg, c.shape[1])))
    return parts[0] if len(parts) == 1 else jnp.concatenate(parts, axis=0)


def _gla_gates(g_ref, params, *_):
    wg_ref, bg_ref, _ = params
    step = g_ref.shape[0]
    block = min(MIX_BLOCK, step)
    row, col = _block_iotas(block)
    tri = (col <= row).astype(BF16)
    w0, w1, _ = _split3(wg_ref[...])
    cum = []
    for b in range(step // block):
        g0, g1, _ = _split3(g_ref[b * block:(b + 1) * block, :])
        x = _dot(g0, w0) + (_dot(g0, w1) + _dot(g1, w0)) + bg_ref[...]
        cum.append(_cumsum_rows(tri, _log_sigmoid(x) * (1.0 / GLA_GATE_NORM)))
    return cum


def _gla_body(p_ref, cum, params, o_ref, state):
    _, _, gain_ref = params
    (st_ref,) = state
    step = o_ref.shape[0]
    block = min(MIX_BLOCK, step)
    off_k, off_v, off_r = GLA_QK, 2 * GLA_QK, 2 * GLA_QK + D_MODEL

    row, col = _block_iotas(block)
    causal = col <= row
    sub_xor = (row // GLA_SUB) ^ (col // GLA_SUB)
    n_levels = (block // GLA_SUB).bit_length() - 1
    gain = gain_ref[...]
    blocks = range(step // block)
    units = [(b, h) for b in blocks for h in range(GLA_HEADS)]

    def rows(b, c0, width):
        return p_ref[b * block:(b + 1) * block, c0:c0 + width]

    c = {u: cum[u[0]][:, u[1] * GLA_DK:(u[1] + 1) * GLA_DK] for u in units}
    q = {u: rows(u[0], u[1] * GLA_DK, GLA_DK) * (GLA_DK ** -0.5) for u in units}
    k = {u: rows(u[0], off_k + u[1] * GLA_DK, GLA_DK) for u in units}
    v = {u: rows(u[0], off_v + u[1] * GLA_DV, GLA_DV).astype(BF16) for u in units}

    ref0 = {u: _row_refs(c[u], block, GLA_SUB, lambda s: None if s == 0 else s - 1) for u in units}
    p = {u: _dot_nt((q[u] * jnp.exp(c[u] - ref0[u])).astype(BF16),
                    (k[u] * jnp.exp(ref0[u] - c[u])).astype(BF16)) for u in units}
    a = {u: jnp.where(causal & (sub_xor == 0), p[u], 0.0) for u in units}
    for lev in range(1, n_levels + 1):
        half = GLA_SUB << (lev - 1)
        sel = causal & (sub_xor >= (1 << (lev - 1))) & (sub_xor < (1 << lev))
        ref = {u: _row_refs(c[u], block, 2 * half, lambda s: s + half - 1) for u in units}
        qs = {u: (q[u] * jnp.exp(jnp.minimum(c[u] - ref[u], 0.0))).astype(BF16) for u in units}
        ks = {u: (k[u] * jnp.exp(jnp.minimum(ref[u] - c[u], 0.0))).astype(BF16) for u in units}
        p = {u: _dot_nt(qs[u], ks[u]) for u in units}
        a = {u: jnp.where(sel, p[u], a[u]) for u in units}
    o_intra = {u: _dot(a[u].astype(BF16), v[u]) for u in units}
    q_dec = {u: (q[u] * jnp.exp(c[u])).astype(BF16) for u in units}
    c_end = {u: c[u][block - 1:block, :] for u in units}
    k_end = {u: (k[u] * jnp.exp(c_end[u] - c[u])).astype(BF16) for u in units}
    kv = {u: _dot_tn(v[u], k_end[u]) for u in units}

    st = [st_ref[h] for h in range(GLA_HEADS)]
    for b in blocks:
        heads = [(b, h) for h in range(GLA_HEADS)]
        o_inter = [_dot_nt(q_dec[u], st[u[1]].astype(BF16)) for u in heads]
        st = [st[u[1]] * jnp.exp(c_end[u]) + kv[u] for u in heads]
        for u, oi in zip(heads, o_inter):
            r = rows(b, off_r + u[1] * GLA_DV, GLA_DV)
            y = _rms(o_intra[u] + oi, gain) * _silu(r)
            o_ref[b * block:(b + 1) * block, u[1] * GLA_DV:(u[1] + 1) * GLA_DV] = y.astype(o_ref.dtype)
    for h in range(GLA_HEADS):
        st_ref[h] = st[h]


def _gla(w, w_g2, b_g2, norm_gain):
    w_g2p = jnp.zeros((N_GATE, GLA_QK), F32).at[LANE_GLR:LANE_GLR + GLA_GATE_RANK].set(w_g2)
    params = (w_g2p, b_g2.reshape(1, GLA_QK), norm_gain.reshape(1, GLA_DV))
    state = (pltpu.VMEM((GLA_HEADS, GLA_DV, GLA_DK), F32),)
    return _Mixer(_gla_gates, _gla_body, w, params, state)


def _mlstm_gates(g_ref, params, *_):
    bias_ref, _ = params
    step = g_ref.shape[0]
    block = min(MIX_BLOCK, step)
    row, col = _block_iotas(block)
    tri = (col <= row).astype(BF16)
    log_i, log_i_t, cum, cum_t = [], [], [], []
    for b in range(step // block):
        capped = ML_GATE_CAP * jnp.tanh((g_ref[b * block:(b + 1) * block, :] + bias_ref[...])
                                        * (1.0 / ML_GATE_CAP))
        cs_b = _cumsum_rows(tri, _log_sigmoid(capped))
        log_i.append(capped)
        log_i_t.append(capped.T)
        cum.append(cs_b)
        cum_t.append(cs_b.T)
    return log_i, log_i_t, cum, cum_t


def _mlstm_body(p_ref, gate_vals, params, o_ref, state):
    _, gain_ref = params
    cs_ref, ns_ref, m_ref = state
    log_i, log_i_t, cum, cum_t = gate_vals
    step = o_ref.shape[0]
    block = min(MIX_BLOCK, step)
    off_k, off_v, off_o = ML_QK, 2 * ML_QK, 2 * ML_QK + D_MODEL

    row, col = _block_iotas(block)
    causal = col <= row
    gain = gain_ref[...]
    blocks = range(step // block)
    heads = range(ML_HEADS)
    units = [(b, h) for b in blocks for h in heads]

    def rows(b, c0, width):
        return p_ref[b * block:(b + 1) * block, c0:c0 + width]

    li_c = {u: log_i[u[0]][:, LANE_ML_I + u[1]:LANE_ML_I + u[1] + 1] for u in units}
    li_r = {u: log_i_t[u[0]][LANE_ML_I + u[1]:LANE_ML_I + u[1] + 1, :] for u in units}
    cum_c = {u: cum[u[0]][:, LANE_ML_F + u[1]:LANE_ML_F + u[1] + 1] for u in units}
    cum_r = {u: cum_t[u[0]][LANE_ML_F + u[1]:LANE_ML_F + u[1] + 1, :] for u in units}
    cum_last = {u: cum_c[u][block - 1:block, :] for u in units}
    log_end = {u: cum_last[u] - cum_c[u] + li_c[u] for u in units}
    m_prev, m_new = {}, {}
    for h in heads:
        m = m_ref[h:h + 1, 0:1]
        for b in blocks:
            u = (b, h)
            m_prev[u] = m
            m = jnp.maximum(cum_last[u] + m, jnp.max(log_end[u], axis=0, keepdims=True))
            m_new[u] = m
        m_ref[h:h + 1, :] = jnp.broadcast_to(m, (1, LANES))

    q = {u: rows(u[0], u[1] * ML_DK, ML_DK) * (ML_DK ** -0.5) for u in units}
    qb = {u: q[u].astype(BF16) for u in units}
    k = {u: rows(u[0], off_k + u[1] * ML_DK, ML_DK) for u in units}
    v = {u: rows(u[0], off_v + u[1] * ML_DV, ML_DV).astype(BF16) for u in units}
    qk = {u: _dot_nt(qb[u], k[u].astype(BF16)) for u in units}
    log_d = {u: jnp.where(causal, cum_c[u] - cum_r[u] + li_r[u], -jnp.inf) for u in units}
    m_inter = {u: cum_c[u] + m_prev[u] for u in units}
    m_t = {u: jnp.maximum(m_inter[u], jnp.max(log_d[u], axis=1, keepdims=True)) for u in units}
    a_inter = {u: jnp.exp(m_inter[u] - m_t[u]) for u in units}
    p = {u: qk[u] * jnp.exp(log_d[u] - m_t[u]) for u in units}
    pv = {u: _dot(p[u].astype(BF16), v[u]) for u in units}
    p_sum = {u: jnp.sum(p[u], axis=1, keepdims=True) for u in units}
    dec = {u: jnp.exp(cum_last[u] + m_prev[u] - m_new[u]) for u in units}
    kw = {u: k[u] * jnp.exp(log_end[u] - m_new[u]) for u in units}
    kwv = {u: _dot_tn(kw[u].astype(BF16), v[u]) for u in units}
    kw_sum = {u: jnp.sum(kw[u], axis=0, keepdims=True) for u in units}

    cs = [cs_ref[h] for h in heads]
    ns = [ns_ref[h:h + 1, :] for h in heads]
    for b in blocks:
        q_cs = [_dot(qb[(b, h)], cs[h].astype(BF16)) for h in heads]
        q_ns = [jnp.sum(q[(b, h)] * ns[h], axis=1, keepdims=True) for h in heads]
        cs = [dec[(b, h)] * cs[h] + kwv[(b, h)] for h in heads]
        ns = [dec[(b, h)] * ns[h] + kw_sum[(b, h)] for h in heads]
        for h in heads:
            u = (b, h)
            num = pv[u] + a_inter[u] * q_cs[h]
            den = p_sum[u] + a_inter[u] * q_ns[h]
            hid = num / jnp.maximum(jnp.abs(den), jnp.exp(-m_t[u]))
            og = rows(b, off_o + h * ML_DV, ML_DV)
            y = _rms(hid, gain) * _sigmoid(og)
            o_ref[b * block:(b + 1) * block, h * ML_DV:(h + 1) * ML_DV] = y.astype(o_ref.dtype)
    for h in heads:
        cs_ref[h] = cs[h]
        ns_ref[h:h + 1, :] = ns[h]


def _mlstm_reset(state):
    cs_ref, ns_ref, m_ref = state
    cs_ref[...] = jnp.zeros_like(cs_ref)
    ns_ref[...] = jnp.zeros_like(ns_ref)
    m_ref[...] = jnp.full_like(m_ref, ML_M_INIT)


def _mlstm(w, i_bias, f_bias, norm_gain):
    bias = jnp.zeros((1, N_GATE), F32)
    bias = bias.at[0, LANE_ML_I:LANE_ML_I + ML_HEADS].set(i_bias)
    bias = bias.at[0, LANE_ML_F:LANE_ML_F + ML_HEADS].set(f_bias)
    params = (bias, norm_gain.reshape(1, ML_DV))
    state = (
        pltpu.VMEM((ML_HEADS, ML_DK, ML_DV), F32),
        pltpu.VMEM((SUBLANES, ML_DK), F32),
        pltpu.VMEM((SUBLANES, LANES), F32),
    )
    return _Mixer(_mlstm_gates, _mlstm_body, w, params, state, reset=_mlstm_reset)


def _gdn_pre(p_cur, p_new, t):
    step = p_cur.shape[0] - SUBLANES
    tail = p_new[step:step + SUBLANES, 0:GDN_QKV]
    p_cur[0:SUBLANES, 0:GDN_QKV] = jnp.where(t == 0, 0.0, tail)


def _gdn_conv(p_ref, conv_ref, qkv_ref, block):
    ext = block + 2 * SUBLANES
    n_delayed = GDN_CONV - 1
    r = lax.broadcasted_iota(jnp.int32, (block, n_delayed * ext), 0)
    c = lax.broadcasted_iota(jnp.int32, (block, n_delayed * ext), 1)
    shift = None
    for j in range(n_delayed):
        hit = c == j * ext + SUBLANES + r - (n_delayed - j)
        shift = hit if shift is None else shift | hit
    shift = shift.astype(BF16)
    for b in range(qkv_ref.shape[0] // block):
        for c0 in range(0, GDN_QKV, CONV_COLS):
            xe = p_ref[b * block:b * block + SUBLANES + block, c0:c0 + CONV_COLS]
            pad = jnp.zeros((SUBLANES, CONV_COLS), F32)
            parts = []
            for j in range(n_delayed):
                parts += [xe * conv_ref[j:j + 1, c0:c0 + CONV_COLS], pad]
            delayed = _dot(shift, jnp.concatenate(parts, axis=0).astype(BF16))
            y = delayed + xe[SUBLANES:, :] * conv_ref[n_delayed:GDN_CONV, c0:c0 + CONV_COLS]
            qkv_ref[b * block:(b + 1) * block, c0:c0 + CONV_COLS] = _silu(y)


def _gdn_gates(g_ref, params, p_ref, state):
    conv_ref, prm_ref, _ = params
    step = g_ref.shape[0]
    block = min(MIX_BLOCK, step)
    _gdn_conv(p_ref, conv_ref, state[1], block)
    row, col = _block_iotas(block)
    tri = (col <= row).astype(BF16)
    cum, cum_t, beta = [], [], []
    for b in range(step // block):
        g = g_ref[b * block:(b + 1) * block, :]
        decay = -jnp.exp(prm_ref[0:1, :]) * _softplus(g + prm_ref[1:2, :])
        cs_b = _cumsum_rows(tri, decay)
        cum.append(cs_b)
        cum_t.append(cs_b.T)
        beta.append(_sigmoid(g))
    return cum, cum_t, beta


def _gdn_body(p_ref, gate_vals, params, o_ref, state):
    _, _, gain_ref = params
    s_ref, qkv_ref = state
    cum, cum_t, beta = gate_vals
    step = o_ref.shape[0]
    block = min(MIX_BLOCK, step)
    off_z = GDN_QKV

    def conv_silu(b, c0):
        return qkv_ref[b * block:(b + 1) * block, c0:c0 + LANES]

    row, col = _block_iotas(block)
    causal = col <= row
    strict = col < row
    eye = (col == row).astype(F32)
    level_masks = [
        ((row >> j == col >> j) & (row >> (j - 1) != col >> (j - 1))).astype(F32)
        for j in range(1, block.bit_length())
    ]
    gain = gain_ref[...]
    blocks = range(step // block)
    heads = range(GDN_HEADS)
    units = [(b, h) for b in blocks for h in heads]

    cum_c = {u: cum[u[0]][:, LANE_GDN_A + u[1]:LANE_GDN_A + u[1] + 1] for u in units}
    cum_r = {u: cum_t[u[0]][LANE_GDN_A + u[1]:LANE_GDN_A + u[1] + 1, :] for u in units}
    beta_c = {u: beta[u[0]][:, LANE_GDN_B + u[1]:LANE_GDN_B + u[1] + 1] for u in units}
    q = {u: conv_silu(u[0], u[1] * GDN_DK) for u in units}
    k = {u: conv_silu(u[0], GDN_HEADS * GDN_DK + u[1] * GDN_DK) for u in units}
    v = {u: conv_silu(u[0], 2 * GDN_HEADS * GDN_DK + u[1] * GDN_DV) for u in units}
    q = {u: q[u] * lax.rsqrt(jnp.sum(q[u] * q[u], axis=-1, keepdims=True) + NORM_EPS) * (GDN_DK ** -0.5)
         for u in units}
    k = {u: k[u] * lax.rsqrt(jnp.sum(k[u] * k[u], axis=-1, keepdims=True) + NORM_EPS) for u in units}
    qb = {u: q[u].astype(BF16) for u in units}
    kb = {u: k[u].astype(BF16) for u in units}
    gamma = {u: jnp.where(causal, jnp.exp(jnp.minimum(cum_c[u] - cum_r[u], 0.0)), 0.0) for u in units}
    kk = {u: _dot_nt(kb[u], kb[u]) for u in units}
    a = {u: jnp.where(strict, beta_c[u] * kk[u] * gamma[u], 0.0) for u in units}
    inv = {u: eye - a[u] * level_masks[0] for u in units}
    for mask in level_masks[1:]:
        xb = {u: inv[u].astype(BF16) for u in units}
        xa = {u: _dot(xb[u], (a[u] * mask).astype(BF16)) for u in units}
        xax = {u: _dot(xa[u].astype(BF16), xb[u]) for u in units}
        inv = {u: inv[u] - xax[u] for u in units}

    e_cum = {u: jnp.exp(cum_c[u]) for u in units}
    rhs = {u: jnp.concatenate([v[u] * beta_c[u], k[u] * (beta_c[u] * e_cum[u])], axis=1).astype(BF16)
           for u in units}
    uw = {u: _dot(inv[u].astype(BF16), rhs[u]) for u in units}
    a_qk = {u: (_dot_nt(qb[u], kb[u]) * gamma[u]).astype(BF16) for u in units}
    q_dec = {u: (q[u] * e_cum[u]).astype(BF16) for u in units}
    cum_last = {u: cum_c[u][block - 1:block, :] for u in units}
    k_end = {u: (k[u] * jnp.exp(cum_last[u] - cum_c[u])).astype(BF16) for u in units}

    st = [s_ref[h] for h in heads]
    for b in blocks:
        us = [(b, h) for h in heads]
        sb = [t_.astype(BF16) for t_ in st]
        ws = [_dot(uw[u][:, GDN_DV:].astype(BF16), sb[u[1]]) for u in us]
        qs = [_dot(q_dec[u], sb[u[1]]) for u in us]
        vb = [(uw[u][:, :GDN_DV] - ws[u[1]]).astype(BF16) for u in us]
        o = [qs[u[1]] + _dot(a_qk[u], vb[u[1]]) for u in us]
        st = [st[u[1]] * jnp.exp(cum_last[u]) + _dot_tn(k_end[u], vb[u[1]]) for u in us]
        for u in us:
            h = u[1]
            z = p_ref[SUBLANES + b * block:SUBLANES + (b + 1) * block, off_z + h * GDN_DV:off_z + (h + 1) * GDN_DV]
            y = _rms(o[h], gain) * _silu(z)
            o_ref[b * block:(b + 1) * block, h * GDN_DV:(h + 1) * GDN_DV] = y.astype(o_ref.dtype)
    for h in heads:
        s_ref[h] = st[h]


def _gdn_reset(state):
    s_ref, _ = state
    s_ref[...] = jnp.zeros_like(s_ref)


def _gdn(w, conv_w, a_log, dt_bias, norm_gain, seq):
    prm = jnp.zeros((SUBLANES, N_GATE), F32)
    prm = prm.at[0, LANE_GDN_A:LANE_GDN_A + GDN_HEADS].set(a_log)
    prm = prm.at[1, LANE_GDN_A:LANE_GDN_A + GDN_HEADS].set(dt_bias)
    params = (conv_w, prm, norm_gain.reshape(1, GDN_DV))
    step = min(GDN_STEP, seq)
    state = (pltpu.VMEM((GDN_HEADS, GDN_DK, GDN_DV), F32), pltpu.VMEM((step, GDN_QKV), F32))
    return _Mixer(_gdn_gates, _gdn_body, w, params, state, reset=_gdn_reset, pre=_gdn_pre, row0=SUBLANES)


def _tail_kernel(yg_ref, ym_ref, yd_ref, x_ref, mod_ref, gmix_ref, gffn_ref, wm_ref, wo_ref, wu_ref,
                 wd_ref, fin_ref, o_ref, *, final_norm):
    x = x_ref[...]
    d = x.shape[1]
    h = _modulated_norm(x, gmix_ref[...], mod_ref[0:1, :], mod_ref[1:2, :]).astype(BF16)
    y = None
    for i, y_ref in enumerate((yg_ref, ym_ref, yd_ref)):
        term = _sigmoid(_dot(h, wm_ref[:, i * d:(i + 1) * d])) * y_ref[...].astype(F32)
        y = term if y is None else y + term
    x = x + mod_ref[2:3, :] * _dot(y.astype(BF16), wo_ref[...])

    h = _modulated_norm(x, gffn_ref[...], mod_ref[3:4, :], mod_ref[4:5, :]).astype(BF16)
    acc = None
    for c in range(D_FF // FF_CHUNK):
        gate = _dot(h, wu_ref[:, c * FF_CHUNK:(c + 1) * FF_CHUNK])
        val = _dot(h, wu_ref[:, D_FF + c * FF_CHUNK:D_FF + (c + 1) * FF_CHUNK])
        part = _dot((_silu(gate) * val).astype(BF16), wd_ref[c * FF_CHUNK:(c + 1) * FF_CHUNK, :])
        acc = part if acc is None else acc + part
    y = x + mod_ref[5:6, :] * acc
    if final_norm:
        y = _rms(y, fin_ref[...])
    o_ref[...] = y


def _tail(y_gla, y_ml, y_gdn, x2, mod, gain_mix, gain_ffn, w_merge, w_out, w_up, w_down, norm_final,
          seq, final_norm):
    t, d = x2.shape
    tm = min(TOK_TM, seq)
    per_seq = seq // tm
    tok = pl.BlockSpec((tm, d), lambda i: (i, 0))

    def resident(shape):
        return pl.BlockSpec(shape, lambda i: (0, 0), pipeline_mode=pl.Buffered(1))

    return pl.pallas_call(
        functools.partial(_tail_kernel, final_norm=final_norm),
        grid=(t // tm,),
        in_specs=[tok, tok, tok, tok,
                  pl.BlockSpec((None, 6, d), lambda i: (i // per_seq, 0, 0)),
                  resident((1, d)), resident((1, d)),
                  resident((d, 3 * d)), resident((d, d)),
                  resident((d, 2 * D_FF)), resident((D_FF, d)),
                  resident((1, d))],
        out_specs=tok,
        out_shape=jax.ShapeDtypeStruct((t, d), F32),
        compiler_params=_params("arbitrary"),
        name="tail",
    )(y_gla, y_ml, y_gdn, x2, mod, gain_mix, gain_ffn, w_merge, w_out, w_up, w_down, norm_final)


def _layout_w_in(w):
    gla_w = 2 * GLA_QK + 2 * D_MODEL
    ml_w = 2 * ML_QK + 2 * D_MODEL
    gdn_w = GDN_QKV + D_MODEL
    o_glr = gla_w
    o_ml = o_glr + GLA_GATE_RANK
    o_mlg = o_ml + ml_w
    o_gdn = o_mlg + 2 * ML_HEADS
    o_gdng = o_gdn + gdn_w
    o_merge = o_gdng + 2 * GDN_HEADS
    small = jnp.concatenate([w[:, o_glr:o_ml], w[:, o_mlg:o_gdn], w[:, o_gdng:o_merge]], axis=1)
    gate = jnp.pad(small, ((0, 0), (0, N_GATE - small.shape[1]))).astype(BF16)
    return (w[:, :gla_w].astype(BF16), w[:, o_ml:o_mlg].astype(BF16), w[:, o_gdn:o_gdng].astype(BF16),
            w[:, o_merge:].astype(BF16), gate)


def kernel(x, c, ada_w, ada_b, norm_mix, norm_ffn, w_in, gla_w_g2, gla_b_g2, gla_norm, ml_i_bias, ml_f_bias, ml_norm, gdn_conv, gdn_a_log, gdn_dt_bias, gdn_norm, w_out, w_ffn_up, w_ffn_down, norm_final):
    b, s, d = x.shape
    depth = ada_w.shape[0]
    mod = _adaln(c, ada_w, ada_b).reshape(depth, b, 6, d)
    x2 = x.reshape(b * s, d)
    fin = norm_final.reshape(1, d)
    for l in range(depth):
        w_gla, w_ml, w_gdn, w_merge, w_gate = _layout_w_in(w_in[l])
        gain = norm_mix[l].reshape(1, d)
        gla = _gla(w_gla, gla_w_g2[l], gla_b_g2[l], gla_norm[l])
        mlstm = _mlstm(w_ml, ml_i_bias[l], ml_f_bias[l], ml_norm[l])
        gdn = _gdn(w_gdn, gdn_conv[l], gdn_a_log[l], gdn_dt_bias[l], gdn_norm[l], s)
        (y_gla,) = _mixer_call("gla", (gla,), x2, mod[l], gain, w_gate, s, MIX_STEP)
        (y_ml,) = _mixer_call("mlstm", (mlstm,), x2, mod[l], gain, w_gate, s, MIX_STEP)
        (y_gdn,) = _mixer_call("gdn", (gdn,), x2, mod[l], gain, w_gate, s, GDN_STEP)
        x2 = _tail(y_gla, y_ml, y_gdn, x2, mod[l], gain, norm_ffn[l].reshape(1, d), w_merge,
                   w_out[l].astype(BF16), w_ffn_up[l].astype(BF16), w_ffn_down[l].astype(BF16), fin, s,
                   final_norm=(l == depth - 1))
    return x2.reshape(b, s, d)
```

```python
import functools
from typing import Callable, NamedTuple

import jax
import jax.numpy as jnp
from jax import lax
from jax.experimental import pallas as pl
from jax.experimental.pallas import tpu as pltpu

F32 = jnp.float32
BF16 = jnp.bfloat16

D_MODEL = 1024
NORM_EPS = 1e-6
GLA_HEADS, GLA_DK, GLA_DV = 4, 128, 256
GLA_QK = GLA_HEADS * GLA_DK
GLA_GATE_RANK = 16
GLA_GATE_NORM = 16.0
GLA_SUB = 16
ML_HEADS, ML_DK, ML_DV = 4, 128, 256
ML_QK = ML_HEADS * ML_DK
ML_GATE_CAP = 15.0
ML_M_INIT = -1e30
GDN_HEADS, GDN_DK, GDN_DV = 8, 128, 128
GDN_CONV = 4
GDN_QKV = GDN_HEADS * (2 * GDN_DK + GDN_DV)
D_FF = 2816
FF_CHUNK = 256

N_GATE = 128
LANE_GLR = 0
LANE_ML_I = 16
LANE_ML_F = 20
LANE_GDN_A = 24
LANE_GDN_B = 32

LANES = 128
SUBLANES = 8
VMEM_LIMIT = 56 * 1024 * 1024

MIX_BLOCK = 128
PAIR_STEP = 256
GDN_STEP = 256
TOK_TM = 512
CONV_COLS = 1024


def _dot(a, b):
    return jnp.dot(a, b, preferred_element_type=F32)


def _dot_nt(a, b):
    return lax.dot_general(a, b, (((1,), (1,)), ((), ())), preferred_element_type=F32)


def _dot_tn(a, b):
    return lax.dot_general(a, b, (((0,), (0,)), ((), ())), preferred_element_type=F32)


def _split3(x):
    hi = x.astype(BF16)
    r = x - hi.astype(F32)
    mid = r.astype(BF16)
    lo = (r - mid.astype(F32)).astype(BF16)
    return hi, mid, lo


def _dot_f32(a, b):
    a0, a1, _ = _split3(a)
    b0, b1, _ = _split3(b)
    return _dot(a0, b0) + (_dot(a0, b1) + _dot(a1, b0))


def _cumsum_rows(tri, x):
    hi, mid, _ = _split3(x)
    return _dot(tri, hi) + _dot(tri, mid)


def _log_sigmoid(x):
    return jnp.minimum(x, 0.0) - jnp.log1p(jnp.exp(-jnp.abs(x)))


def _softplus(x):
    return jnp.maximum(x, 0.0) + jnp.log1p(jnp.exp(-jnp.abs(x)))


def _sigmoid(x):
    return 1.0 / (1.0 + jnp.exp(-x))


def _silu(x):
    return x * _sigmoid(x)


def _rms(x, gain):
    return x * lax.rsqrt(jnp.mean(x * x, axis=-1, keepdims=True) + NORM_EPS) * gain


def _modulated_norm(x, gain, shift, scale):
    return _rms(x, gain) * (1.0 + scale) + shift


def _params(*sem):
    return pltpu.CompilerParams(dimension_semantics=sem, vmem_limit_bytes=VMEM_LIMIT)


def _block_iotas(block):
    row = lax.broadcasted_iota(jnp.int32, (block, block), 0)
    col = lax.broadcasted_iota(jnp.int32, (block, block), 1)
    return row, col


def _adaln_kernel(c_ref, w_ref, b_ref, o_ref):
    c = c_ref[...]
    o_ref[...] = _dot_f32(_silu(c), w_ref[...]) + b_ref[...]


def _adaln(c, ada_w, ada_b):
    depth, d, n = ada_w.shape
    b = c.shape[0]
    tn = D_MODEL
    return pl.pallas_call(
        _adaln_kernel,
        grid=(depth, n // tn),
        in_specs=[
            pl.BlockSpec((b, d), lambda l, j: (0, 0)),
            pl.BlockSpec((None, d, tn), lambda l, j: (l, 0, j)),
            pl.BlockSpec((None, 1, tn), lambda l, j: (l, 0, j)),
        ],
        out_specs=pl.BlockSpec((None, b, tn), lambda l, j: (l, 0, j)),
        out_shape=jax.ShapeDtypeStruct((depth, b, n), F32),
        compiler_params=_params("arbitrary", "arbitrary"),
        name="adaln",
    )(c, ada_w, ada_b.reshape(depth, 1, n))


def _zero_state(state):
    for ref in state:
        ref[...] = jnp.zeros_like(ref)


class _Mixer(NamedTuple):
    gates: Callable
    body: Callable
    w: jax.Array
    params: tuple
    state_shapes: tuple
    reset: Callable = _zero_state
    pre: Callable | None = None
    row0: int = 0


def _mixer_kernel(x_ref, mod_ref, gain_ref, wg_ref, *rest, mixers, steps_per_seq):
    n = len(mixers)
    pos = 0
    w_refs, params = [], []
    for m in mixers:
        w_refs.append(rest[pos])
        params.append(rest[pos + 1:pos + 1 + len(m.params)])
        pos += 1 + len(m.params)
    o_refs = rest[pos:pos + n]
    g_refs = rest[pos + n:pos + n + 2]
    pos += n + 2
    p_refs, states = [], []
    for m in mixers:
        p_refs.append(rest[pos:pos + 2])
        states.append(rest[pos + 2:pos + 2 + len(m.state_shapes)])
        pos += 2 + len(m.state_shapes)
    s = pl.program_id(0)
    step = x_ref.shape[0]

    @pl.when(s == 0)
    def _():
        for ref in (*g_refs, *[r for pr in p_refs for r in pr], *[r for st in states for r in st]):
            ref[...] = jnp.zeros_like(ref)

    t = (s + steps_per_seq - 1) % steps_per_seq

    @pl.when(t == 0)
    def _():
        for m, st in zip(mixers, states):
            m.reset(st)

    for parity in (0, 1):
        @pl.when(s % 2 == parity)
        def _(parity=parity):
            gate_vals = []
            for i, m in enumerate(mixers):
                p_new, p_cur = p_refs[i][parity], p_refs[i][1 - parity]
                if m.pre is not None:
                    m.pre(p_cur, p_new, t)
                gate_vals.append(m.gates(g_refs[1 - parity], params[i], p_cur, states[i]))
            h = _modulated_norm(x_ref[...], gain_ref[...], mod_ref[0:1, :], mod_ref[1:2, :]).astype(BF16)
            for i, m in enumerate(mixers):
                p_refs[i][parity][m.row0:m.row0 + step, :] = _dot(h, w_refs[i][...])
            g_refs[parity][...] = _dot(h, wg_ref[...])
            for i, m in enumerate(mixers):
                m.body(p_refs[i][1 - parity], gate_vals[i], params[i], o_refs[i], states[i])


def _mixer_call(name, mixers, x2, mod, gain, w_gate, seq, step):
    t, d = x2.shape
    step = min(step, seq)
    n = t // step
    per_seq = seq // step

    def cur(s):
        return jnp.minimum(s, n - 1)

    operands, specs, scratch = [], [], [pltpu.VMEM((step, N_GATE), F32)] * 2
    for m in mixers:
        operands += [m.w, *m.params]
        specs += [pl.BlockSpec(m.w.shape, lambda s: (0, 0), pipeline_mode=pl.Buffered(1)),
                  *[_full_spec(p.shape) for p in m.params]]
        scratch += [pltpu.VMEM((m.row0 + step, m.w.shape[1]), F32)] * 2 + list(m.state_shapes)
    out_spec = pl.BlockSpec((step, d), lambda s: (jnp.maximum(s - 1, 0), 0))
    return pl.pallas_call(
        functools.partial(_mixer_kernel, steps_per_seq=per_seq,
                          mixers=tuple(m._replace(w=None, params=(None,) * len(m.params)) for m in mixers)),
        grid=(n + 1,),
        in_specs=[
            pl.BlockSpec((step, d), lambda s: (cur(s), 0)),
            pl.BlockSpec((None, 6, d), lambda s: (cur(s) // per_seq, 0, 0)),
            pl.BlockSpec((1, d), lambda s: (0, 0)),
            pl.BlockSpec((d, N_GATE), lambda s: (0, 0)),
            *specs,
        ],
        out_specs=[out_spec] * len(mixers),
        out_shape=[jax.ShapeDtypeStruct((t, d), BF16)] * len(mixers),
        scratch_shapes=scratch,
        compiler_params=_params("arbitrary"),
        name=name,
    )(x2, mod, gain, w_gate, *operands)


def _full_spec(shape):
    return pl.BlockSpec(shape, lambda s: (0,) * len(shape))


def _row_refs(c, block, seg, pick):
    parts = []
    for start in range(0, block, seg):
        r = pick(start)
        row = jnp.zeros((1, c.shape[1]), F32) if r is None else c[r:r + 1, :]
        parts.append(jnp.broadcast_to(row, (seg, c.shape[1])))
    return parts[0] if len(parts) == 1 else jnp.concatenate(parts, axis=0)


def _gla_gates(g_ref, params, *_):
    wg_ref, bg_ref, _ = params
    step = g_ref.shape[0]
    block = min(MIX_BLOCK, step)
    row, col = _block_iotas(block)
    tri = (col <= row).astype(BF16)
    w0, w1, _ = _split3(wg_ref[...])
    cum = []
    for b in range(step // block):
        g0, g1, _ = _split3(g_ref[b * block:(b + 1) * block, :])
        x = _dot(g0, w0) + (_dot(g0, w1) + _dot(g1, w0)) + bg_ref[...]
        cum.append(_cumsum_rows(tri, _log_sigmoid(x) * (1.0 / GLA_GATE_NORM)))
    return cum


def _gla_body(p_ref, cum, params, o_ref, state):
    _, _, gain_ref = params
    (st_ref,) = state
    step = o_ref.shape[0]
    block = min(MIX_BLOCK, step)
    off_k, off_v, off_r = GLA_QK, 2 * GLA_QK, 2 * GLA_QK + D_MODEL

    row, col = _block_iotas(block)
    causal = col <= row
    sub_xor = (row // GLA_SUB) ^ (col // GLA_SUB)
    n_levels = (block // GLA_SUB).bit_length() - 1
    gain = gain_ref[...]
    blocks = range(step // block)
    units = [(b, h) for b in blocks for h in range(GLA_HEADS)]

    def rows(b, c0, width):
        return p_ref[b * block:(b + 1) * block, c0:c0 + width]

    c = {u: cum[u[0]][:, u[1] * GLA_DK:(u[1] + 1) * GLA_DK] for u in units}
    q = {u: rows(u[0], u[1] * GLA_DK, GLA_DK) * (GLA_DK ** -0.5) for u in units}
    k = {u: rows(u[0], off_k + u[1] * GLA_DK, GLA_DK) for u in units}
    v = {u: rows(u[0], off_v + u[1] * GLA_DV, GLA_DV).astype(BF16) for u in units}

    ref0 = {u: _row_refs(c[u], block, GLA_SUB, lambda s: None if s == 0 else s - 1) for u in units}
    p = {u: _dot_nt((q[u] * jnp.exp(c[u] - ref0[u])).astype(BF16),
                    (k[u] * jnp.exp(ref0[u] - c[u])).astype(BF16)) for u in units}
    a = {u: jnp.where(causal & (sub_xor == 0), p[u], 0.0) for u in units}
    for lev in range(1, n_levels + 1):
        half = GLA_SUB << (lev - 1)
        sel = causal & (sub_xor >= (1 << (lev - 1))) & (sub_xor < (1 << lev))
        ref = {u: _row_refs(c[u], block, 2 * half, lambda s: s + half - 1) for u in units}
        qs = {u: (q[u] * jnp.exp(jnp.minimum(c[u] - ref[u], 0.0))).astype(BF16) for u in units}
        ks = {u: (k[u] * jnp.exp(jnp.minimum(ref[u] - c[u], 0.0))).astype(BF16) for u in units}
        p = {u: _dot_nt(qs[u], ks[u]) for u in units}
        a = {u: jnp.where(sel, p[u], a[u]) for u in units}
    o_intra = {u: _dot(a[u].astype(BF16), v[u]) for u in units}
    q_dec = {u: (q[u] * jnp.exp(c[u])).astype(BF16) for u in units}
    c_end = {u: c[u][block - 1:block, :] for u in units}
    k_end = {u: (k[u] * jnp.exp(c_end[u] - c[u])).astype(BF16) for u in units}
    kv = {u: _dot_tn(v[u], k_end[u]) for u in units}

    st = [st_ref[h] for h in range(GLA_HEADS)]
    for b in blocks:
        heads = [(b, h) for h in range(GLA_HEADS)]
        o_inter = [_dot_nt(q_dec[u], st[u[1]].astype(BF16)) for u in heads]
        st = [st[u[1]] * jnp.exp(c_end[u]) + kv[u] for u in heads]
        for u, oi in zip(heads, o_inter):
            r = rows(b, off_r + u[1] * GLA_DV, GLA_DV)
            y = _rms(o_intra[u] + oi, gain) * _silu(r)
            o_ref[b * block:(b + 1) * block, u[1] * GLA_DV:(u[1] + 1) * GLA_DV] = y.astype(o_ref.dtype)
    for h in range(GLA_HEADS):
        st_ref[h] = st[h]


def _gla(w, w_g2, b_g2, norm_gain):
    w_g2p = jnp.zeros((N_GATE, GLA_QK), F32).at[LANE_GLR:LANE_GLR + GLA_GATE_RANK].set(w_g2)
    params = (w_g2p, b_g2.reshape(1, GLA_QK), norm_gain.reshape(1, GLA_DV))
    state = (pltpu.VMEM((GLA_HEADS, GLA_DV, GLA_DK), F32),)
    return _Mixer(_gla_gates, _gla_body, w, params, state)


def _mlstm_gates(g_ref, params, *_):
    bias_ref, _ = params
    step = g_ref.shape[0]
    block = min(MIX_BLOCK, step)
    row, col = _block_iotas(block)
    tri = (col <= row).astype(BF16)
    log_i, log_i_t, cum, cum_t = [], [], [], []
    for b in range(step // block):
        capped = ML_GATE_CAP * jnp.tanh((g_ref[b * block:(b + 1) * block, :] + bias_ref[...])
                                        * (1.0 / ML_GATE_CAP))
        cs_b = _cumsum_rows(tri, _log_sigmoid(capped))
        log_i.append(capped)
        log_i_t.append(capped.T)
        cum.append(cs_b)
        cum_t.append(cs_b.T)
    return log_i, log_i_t, cum, cum_t


def _mlstm_body(p_ref, gate_vals, params, o_ref, state):
    _, gain_ref = params
    cs_ref, ns_ref, m_ref = state
    log_i, log_i_t, cum, cum_t = gate_vals
    step = o_ref.shape[0]
    block = min(MIX_BLOCK, step)
    off_k, off_v, off_o = ML_QK, 2 * ML_QK, 2 * ML_QK + D_MODEL

    row, col = _block_iotas(block)
    causal = col <= row
    gain = gain_ref[...]
    blocks = range(step // block)
    heads = range(ML_HEADS)
    units = [(b, h) for b in blocks for h in heads]

    def rows(b, c0, width):
        return p_ref[b * block:(b + 1) * block, c0:c0 + width]

    li_c = {u: log_i[u[0]][:, LANE_ML_I + u[1]:LANE_ML_I + u[1] + 1] for u in units}
    li_r = {u: log_i_t[u[0]][LANE_ML_I + u[1]:LANE_ML_I + u[1] + 1, :] for u in units}
    cum_c = {u: cum[u[0]][:, LANE_ML_F + u[1]:LANE_ML_F + u[1] + 1] for u in units}
    cum_r = {u: cum_t[u[0]][LANE_ML_F + u[1]:LANE_ML_F + u[1] + 1, :] for u in units}
    cum_last = {u: cum_c[u][block - 1:block, :] for u in units}
    log_end = {u: cum_last[u] - cum_c[u] + li_c[u] for u in units}
    m_prev, m_new = {}, {}
    for h in heads:
        m = m_ref[h:h + 1, 0:1]
        for b in blocks:
            u = (b, h)
            m_prev[u] = m
            m = jnp.maximum(cum_last[u] + m, jnp.max(log_end[u], axis=0, keepdims=True))
            m_new[u] = m
        m_ref[h:h + 1, :] = jnp.broadcast_to(m, (1, LANES))

    q = {u: rows(u[0], u[1] * ML_DK, ML_DK) * (ML_DK ** -0.5) for u in units}
    qb = {u: q[u].astype(BF16) for u in units}
    k = {u: rows(u[0], off_k + u[1] * ML_DK, ML_DK) for u in units}
    v = {u: rows(u[0], off_v + u[1] * ML_DV, ML_DV).astype(BF16) for u in units}
    qk = {u: _dot_nt(qb[u], k[u].astype(BF16)) for u in units}
    log_d = {u: jnp.where(causal, cum_c[u] - cum_r[u] + li_r[u], -jnp.inf) for u in units}
    m_inter = {u: cum_c[u] + m_prev[u] for u in units}
    m_t = {u: jnp.maximum(m_inter[u], jnp.max(log_d[u], axis=1, keepdims=True)) for u in units}
    a_inter = {u: jnp.exp(m_inter[u] - m_t[u]) for u in units}
    p = {u: qk[u] * jnp.exp(log_d[u] - m_t[u]) for u in units}
    pv = {u: _dot(p[u].astype(BF16), v[u]) for u in units}
    p_sum = {u: jnp.sum(p[u], axis=1, keepdims=True) for u in units}
    dec = {u: jnp.exp(cum_last[u] + m_prev[u] - m_new[u]) for u in units}
    kw = {u: k[u] * jnp.exp(log_end[u] - m_new[u]) for u in units}
    kwv = {u: _dot_tn(kw[u].astype(BF16), v[u]) for u in units}
    kw_sum = {u: jnp.sum(kw[u], axis=0, keepdims=True) for u in units}

    cs = [cs_ref[h] for h in heads]
    ns = [ns_ref[h:h + 1, :] for h in heads]
    for b in blocks:
        q_cs = [_dot(qb[(b, h)], cs[h].astype(BF16)) for h in heads]
        q_ns = [jnp.sum(q[(b, h)] * ns[h], axis=1, keepdims=True) for h in heads]
        cs = [dec[(b, h)] * cs[h] + kwv[(b, h)] for h in heads]
        ns = [dec[(b, h)] * ns[h] + kw_sum[(b, h)] for h in heads]
        for h in heads:
            u = (b, h)
            num = pv[u] + a_inter[u] * q_cs[h]
            den = p_sum[u] + a_inter[u] * q_ns[h]
            hid = num / jnp.maximum(jnp.abs(den), jnp.exp(-m_t[u]))
            og = rows(b, off_o + h * ML_DV, ML_DV)
            y = _rms(hid, gain) * _sigmoid(og)
            o_ref[b * block:(b + 1) * block, h * ML_DV:(h + 1) * ML_DV] = y.astype(o_ref.dtype)
    for h in heads:
        cs_ref[h] = cs[h]
        ns_ref[h:h + 1, :] = ns[h]


def _mlstm_reset(state):
    cs_ref, ns_ref, m_ref = state
    cs_ref[...] = jnp.zeros_like(cs_ref)
    ns_ref[...] = jnp.zeros_like(ns_ref)
    m_ref[...] = jnp.full_like(m_ref, ML_M_INIT)


def _mlstm(w, i_bias, f_bias, norm_gain):
    bias = jnp.zeros((1, N_GATE), F32)
    bias = bias.at[0, LANE_ML_I:LANE_ML_I + ML_HEADS].set(i_bias)
    bias = bias.at[0, LANE_ML_F:LANE_ML_F + ML_HEADS].set(f_bias)
    params = (bias, norm_gain.reshape(1, ML_DV))
    state = (
        pltpu.VMEM((ML_HEADS, ML_DK, ML_DV), F32),
        pltpu.VMEM((SUBLANES, ML_DK), F32),
        pltpu.VMEM((SUBLANES, LANES), F32),
    )
    return _Mixer(_mlstm_gates, _mlstm_body, w, params, state, reset=_mlstm_reset)


def _gdn_pre(p_cur, p_new, t):
    step = p_cur.shape[0] - SUBLANES
    tail = p_new[step:step + SUBLANES, 0:GDN_QKV]
    p_cur[0:SUBLANES, 0:GDN_QKV] = jnp.where(t == 0, 0.0, tail)


def _gdn_conv(p_ref, conv_ref, qkv_ref, block):
    ext = block + 2 * SUBLANES
    n_delayed = GDN_CONV - 1
    r = lax.broadcasted_iota(jnp.int32, (block, n_delayed * ext), 0)
    c = lax.broadcasted_iota(jnp.int32, (block, n_delayed * ext), 1)
    shift = None
    for j in range(n_delayed):
        hit = c == j * ext + SUBLANES + r - (n_delayed - j)
        shift = hit if shift is None else shift | hit
    shift = shift.astype(BF16)
    for b in range(qkv_ref.shape[0] // block):
        for c0 in range(0, GDN_QKV, CONV_COLS):
            xe = p_ref[b * block:b * block + SUBLANES + block, c0:c0 + CONV_COLS]
            pad = jnp.zeros((SUBLANES, CONV_COLS), F32)
            parts = []
            for j in range(n_delayed):
                parts += [xe * conv_ref[j:j + 1, c0:c0 + CONV_COLS], pad]
            delayed = _dot(shift, jnp.concatenate(parts, axis=0).astype(BF16))
            y = delayed + xe[SUBLANES:, :] * conv_ref[n_delayed:GDN_CONV, c0:c0 + CONV_COLS]
            qkv_ref[b * block:(b + 1) * block, c0:c0 + CONV_COLS] = _silu(y)


def _gdn_gates(g_ref, params, p_ref, state):
    conv_ref, prm_ref, _ = params
    step = g_ref.shape[0]
    block = min(MIX_BLOCK, step)
    _gdn_conv(p_ref, conv_ref, state[1], block)
    row, col = _block_iotas(block)
    tri = (col <= row).astype(BF16)
    cum, cum_t, beta = [], [], []
    for b in range(step // block):
        g = g_ref[b * block:(b + 1) * block, :]
        decay = -jnp.exp(prm_ref[0:1, :]) * _softplus(g + prm_ref[1:2, :])
        cs_b = _cumsum_rows(tri, decay)
        cum.append(cs_b)
        cum_t.append(cs_b.T)
        beta.append(_sigmoid(g))
    return cum, cum_t, beta


def _gdn_body(p_ref, gate_vals, params, o_ref, state):
    _, _, gain_ref = params
    s_ref, qkv_ref = state
    cum, cum_t, beta = gate_vals
    step = o_ref.shape[0]
    block = min(MIX_BLOCK, step)
    off_z = GDN_QKV

    def conv_silu(b, c0):
        return qkv_ref[b * block:(b + 1) * block, c0:c0 + LANES]

    row, col = _block_iotas(block)
    causal = col <= row
    strict = col < row
    eye = (col == row).astype(F32)
    level_masks = [
        ((row >> j == col >> j) & (row >> (j - 1) != col >> (j - 1))).astype(F32)
        for j in range(1, block.bit_length())
    ]
    gain = gain_ref[...]
    blocks = range(step // block)
    heads = range(GDN_HEADS)
    units = [(b, h) for b in blocks for h in heads]

    cum_c = {u: cum[u[0]][:, LANE_GDN_A + u[1]:LANE_GDN_A + u[1] + 1] for u in units}
    cum_r = {u: cum_t[u[0]][LANE_GDN_A + u[1]:LANE_GDN_A + u[1] + 1, :] for u in units}
    beta_c = {u: beta[u[0]][:, LANE_GDN_B + u[1]:LANE_GDN_B + u[1] + 1] for u in units}
    q = {u: conv_silu(u[0], u[1] * GDN_DK) for u in units}
    k = {u: conv_silu(u[0], GDN_HEADS * GDN_DK + u[1] * GDN_DK) for u in units}
    v = {u: conv_silu(u[0], 2 * GDN_HEADS * GDN_DK + u[1] * GDN_DV) for u in units}
    q = {u: q[u] * lax.rsqrt(jnp.sum(q[u] * q[u], axis=-1, keepdims=True) + NORM_EPS) * (GDN_DK ** -0.5)
         for u in units}
    k = {u: k[u] * lax.rsqrt(jnp.sum(k[u] * k[u], axis=-1, keepdims=True) + NORM_EPS) for u in units}
    qb = {u: q[u].astype(BF16) for u in units}
    kb = {u: k[u].astype(BF16) for u in units}
    gamma = {u: jnp.where(causal, jnp.exp(jnp.minimum(cum_c[u] - cum_r[u], 0.0)), 0.0) for u in units}
    kk = {u: _dot_nt(kb[u], kb[u]) for u in units}
    a = {u: jnp.where(strict, beta_c[u] * kk[u] * gamma[u], 0.0) for u in units}
    inv = {u: eye - a[u] * level_masks[0] for u in units}
    for mask in level_masks[1:]:
        xb = {u: inv[u].astype(BF16) for u in units}
        xa = {u: _dot(xb[u], (a[u] * mask).astype(BF16)) for u in units}
        xax = {u: _dot(xa[u].astype(BF16), xb[u]) for u in units}
        inv = {u: inv[u] - xax[u] for u in units}

    e_cum = {u: jnp.exp(cum_c[u]) for u in units}
    rhs = {u: jnp.concatenate([v[u] * beta_c[u], k[u] * (beta_c[u] * e_cum[u])], axis=1).astype(BF16)
           for u in units}
    uw = {u: _dot(inv[u].astype(BF16), rhs[u]) for u in units}
    a_qk = {u: (_dot_nt(qb[u], kb[u]) * gamma[u]).astype(BF16) for u in units}
    q_dec = {u: (q[u] * e_cum[u]).astype(BF16) for u in units}
    cum_last = {u: cum_c[u][block - 1:block, :] for u in units}
    k_end = {u: (k[u] * jnp.exp(cum_last[u] - cum_c[u])).astype(BF16) for u in units}

    st = [s_ref[h] for h in heads]
    for b in blocks:
        us = [(b, h) for h in heads]
        sb = [t_.astype(BF16) for t_ in st]
        ws = [_dot(uw[u][:, GDN_DV:].astype(BF16), sb[u[1]]) for u in us]
        qs = [_dot(q_dec[u], sb[u[1]]) for u in us]
        vb = [(uw[u][:, :GDN_DV] - ws[u[1]]).astype(BF16) for u in us]
        o = [qs[u[1]] + _dot(a_qk[u], vb[u[1]]) for u in us]
        st = [st[u[1]] * jnp.exp(cum_last[u]) + _dot_tn(k_end[u], vb[u[1]]) for u in us]
        for u in us:
            h = u[1]
            z = p_ref[SUBLANES + b * block:SUBLANES + (b + 1) * block, off_z + h * GDN_DV:off_z + (h + 1) * GDN_DV]
            y = _rms(o[h], gain) * _silu(z)
            o_ref[b * block:(b + 1) * block, h * GDN_DV:(h + 1) * GDN_DV] = y.astype(o_ref.dtype)
    for h in heads:
        s_ref[h] = st[h]


def _gdn_reset(state):
    s_ref, _ = state
    s_ref[...] = jnp.zeros_like(s_ref)


def _gdn(w, conv_w, a_log, dt_bias, norm_gain, seq):
    prm = jnp.zeros((SUBLANES, N_GATE), F32)
    prm = prm.at[0, LANE_GDN_A:LANE_GDN_A + GDN_HEADS].set(a_log)
    prm = prm.at[1, LANE_GDN_A:LANE_GDN_A + GDN_HEADS].set(dt_bias)
    params = (conv_w, prm, norm_gain.reshape(1, GDN_DV))
    step = min(GDN_STEP, seq)
    state = (pltpu.VMEM((GDN_HEADS, GDN_DK, GDN_DV), F32), pltpu.VMEM((step, GDN_QKV), F32))
    return _Mixer(_gdn_gates, _gdn_body, w, params, state, reset=_gdn_reset, pre=_gdn_pre, row0=SUBLANES)


def _tail_kernel(yg_ref, ym_ref, yd_ref, x_ref, mod_ref, gmix_ref, gffn_ref, wm_ref, wo_ref, wu_ref,
                 wd_ref, fin_ref, o_ref, *, final_norm):
    x = x_ref[...]
    d = x.shape[1]
    h = _modulated_norm(x, gmix_ref[...], mod_ref[0:1, :], mod_ref[1:2, :]).astype(BF16)
    y = None
    for i, y_ref in enumerate((yg_ref, ym_ref, yd_ref)):
        term = _sigmoid(_dot(h, wm_ref[:, i * d:(i + 1) * d])) * y_ref[...].astype(F32)
        y = term if y is None else y + term
    x = x + mod_ref[2:3, :] * _dot(y.astype(BF16), wo_ref[...])

    h = _modulated_norm(x, gffn_ref[...], mod_ref[3:4, :], mod_ref[4:5, :]).astype(BF16)
    acc = None
    for c in range(D_FF // FF_CHUNK):
        gate = _dot(h, wu_ref[:, c * FF_CHUNK:(c + 1) * FF_CHUNK])
        val = _dot(h, wu_ref[:, D_FF + c * FF_CHUNK:D_FF + (c + 1) * FF_CHUNK])
        part = _dot((_silu(gate) * val).astype(BF16), wd_ref[c * FF_CHUNK:(c + 1) * FF_CHUNK, :])
        acc = part if acc is None else acc + part
    y = x + mod_ref[5:6, :] * acc
    if final_norm:
        y = _rms(y, fin_ref[...])
    o_ref[...] = y


def _tail(y_gla, y_ml, y_gdn, x2, mod, gain_mix, gain_ffn, w_merge, w_out, w_up, w_down, norm_final,
          seq, final_norm):
    t, d = x2.shape
    tm = min(TOK_TM, seq)
    per_seq = seq // tm
    tok = pl.BlockSpec((tm, d), lambda i: (i, 0))

    def resident(shape):
        return pl.BlockSpec(shape, lambda i: (0, 0), pipeline_mode=pl.Buffered(1))

    return pl.pallas_call(
        functools.partial(_tail_kernel, final_norm=final_norm),
        grid=(t // tm,),
        in_specs=[tok, tok, tok, tok,
                  pl.BlockSpec((None, 6, d), lambda i: (i // per_seq, 0, 0)),
                  resident((1, d)), resident((1, d)),
                  resident((d, 3 * d)), resident((d, d)),
                  resident((d, 2 * D_FF)), resident((D_FF, d)),
                  resident((1, d))],
        out_specs=tok,
        out_shape=jax.ShapeDtypeStruct((t, d), F32),
        compiler_params=_params("arbitrary"),
        name="tail",
    )(y_gla, y_ml, y_gdn, x2, mod, gain_mix, gain_ffn, w_merge, w_out, w_up, w_down, norm_final)


def _layout_w_in(w):
    gla_w = 2 * GLA_QK + 2 * D_MODEL
    ml_w = 2 * ML_QK + 2 * D_MODEL
    gdn_w = GDN_QKV + D_MODEL
    o_glr = gla_w
    o_ml = o_glr + GLA_GATE_RANK
    o_mlg = o_ml + ml_w
    o_gdn = o_mlg + 2 * ML_HEADS
    o_gdng = o_gdn + gdn_w
    o_merge = o_gdng + 2 * GDN_HEADS
    small = jnp.concatenate([w[:, o_glr:o_ml], w[:, o_mlg:o_gdn], w[:, o_gdng:o_merge]], axis=1)
    gate = jnp.pad(small, ((0, 0), (0, N_GATE - small.shape[1]))).astype(BF16)
    return (w[:, :gla_w].astype(BF16), w[:, o_ml:o_mlg].astype(BF16), w[:, o_gdn:o_gdng].astype(BF16),
            w[:, o_merge:].astype(BF16), gate)


def kernel(x, c, ada_w, ada_b, norm_mix, norm_ffn, w_in, gla_w_g2, gla_b_g2, gla_norm, ml_i_bias, ml_f_bias, ml_norm, gdn_conv, gdn_a_log, gdn_dt_bias, gdn_norm, w_out, w_ffn_up, w_ffn_down, norm_final):
    b, s, d = x.shape
    depth = ada_w.shape[0]
    mod = _adaln(c, ada_w, ada_b).reshape(depth, b, 6, d)
    x2 = x.reshape(b * s, d)
    fin = norm_final.reshape(1, d)
    for l in range(depth):
        w_gla, w_ml, w_gdn, w_merge, w_gate = _layout_w_in(w_in[l])
        gain = norm_mix[l].reshape(1, d)
        gla = _gla(w_gla, gla_w_g2[l], gla_b_g2[l], gla_norm[l])
        mlstm = _mlstm(w_ml, ml_i_bias[l], ml_f_bias[l], ml_norm[l])
        gdn = _gdn(w_gdn, gdn_conv[l], gdn_a_log[l], gdn_dt_bias[l], gdn_norm[l], s)
        y_gla, y_ml = _mixer_call("gla_mlstm", (gla, mlstm), x2, mod[l], gain, w_gate, s, PAIR_STEP)
        (y_gdn,) = _mixer_call("gdn", (gdn,), x2, mod[l], gain, w_gate, s, GDN_STEP)
        x2 = _tail(y_gla, y_ml, y_gdn, x2, mod[l], gain, norm_ffn[l].reshape(1, d), w_merge,
                   w_out[l].astype(BF16), w_ffn_up[l].astype(BF16), w_ffn_down[l].astype(BF16), fin, s,
                   final_norm=(l == depth - 1))
    return x2.reshape(b, s, d)
```

```python
import functools
from typing import Callable, NamedTuple

import jax
import jax.numpy as jnp
from jax import lax
from jax.experimental import pallas as pl
from jax.experimental.pallas import tpu as pltpu

F32 = jnp.float32
BF16 = jnp.bfloat16

D_MODEL = 1024
NORM_EPS = 1e-6
GLA_HEADS, GLA_DK, GLA_DV = 4, 128, 256
GLA_QK = GLA_HEADS * GLA_DK
GLA_GATE_RANK = 16
GLA_GATE_NORM = 16.0
GLA_SUB = 16
ML_HEADS, ML_DK, ML_DV = 4, 128, 256
ML_QK = ML_HEADS * ML_DK
ML_GATE_CAP = 15.0
ML_M_INIT = -1e30
GDN_HEADS, GDN_DK, GDN_DV = 8, 128, 128
GDN_CONV = 4
GDN_QKV = GDN_HEADS * (2 * GDN_DK + GDN_DV)
D_FF = 2816
FF_CHUNK = 256

N_GATE = 128
LANE_GLR = 0
LANE_ML_I = 16
LANE_ML_F = 20
LANE_GDN_A = 24
LANE_GDN_B = 32

LANES = 128
SUBLANES = 8
VMEM_LIMIT = 56 * 1024 * 1024

MIX_BLOCK = 128
GLA_STEP = 512
GDN_STEP = 256
TOK_TM = 512
CONV_COLS = 1024


def _dot(a, b):
    return jnp.dot(a, b, preferred_element_type=F32)


def _dot_nt(a, b):
    return lax.dot_general(a, b, (((1,), (1,)), ((), ())), preferred_element_type=F32)


def _dot_tn(a, b):
    return lax.dot_general(a, b, (((0,), (0,)), ((), ())), preferred_element_type=F32)


def _split3(x):
    hi = x.astype(BF16)
    r = x - hi.astype(F32)
    mid = r.astype(BF16)
    lo = (r - mid.astype(F32)).astype(BF16)
    return hi, mid, lo


def _dot_f32(a, b):
    a0, a1, _ = _split3(a)
    b0, b1, _ = _split3(b)
    return _dot(a0, b0) + (_dot(a0, b1) + _dot(a1, b0))


def _cumsum_rows(tri, x):
    hi, mid, _ = _split3(x)
    return _dot(tri, hi) + _dot(tri, mid)


def _log_sigmoid(x):
    return jnp.minimum(x, 0.0) - jnp.log1p(jnp.exp(-jnp.abs(x)))


def _softplus(x):
    return jnp.maximum(x, 0.0) + jnp.log1p(jnp.exp(-jnp.abs(x)))


def _sigmoid(x):
    return 1.0 / (1.0 + jnp.exp(-x))


def _silu(x):
    return x * _sigmoid(x)


def _rms(x, gain):
    return x * lax.rsqrt(jnp.mean(x * x, axis=-1, keepdims=True) + NORM_EPS) * gain


def _modulated_norm(x, gain, shift, scale):
    return _rms(x, gain) * (1.0 + scale) + shift


def _params(*sem):
    return pltpu.CompilerParams(dimension_semantics=sem, vmem_limit_bytes=VMEM_LIMIT)


def _block_iotas(block):
    row = lax.broadcasted_iota(jnp.int32, (block, block), 0)
    col = lax.broadcasted_iota(jnp.int32, (block, block), 1)
    return row, col


def _adaln_kernel(c_ref, w_ref, b_ref, o_ref):
    c = c_ref[...]
    o_ref[...] = _dot_f32(_silu(c), w_ref[...]) + b_ref[...]


def _adaln(c, ada_w, ada_b):
    depth, d, n = ada_w.shape
    b = c.shape[0]
    tn = D_MODEL
    return pl.pallas_call(
        _adaln_kernel,
        grid=(depth, n // tn),
        in_specs=[
            pl.BlockSpec((b, d), lambda l, j: (0, 0)),
            pl.BlockSpec((None, d, tn), lambda l, j: (l, 0, j)),
            pl.BlockSpec((None, 1, tn), lambda l, j: (l, 0, j)),
        ],
        out_specs=pl.BlockSpec((None, b, tn), lambda l, j: (l, 0, j)),
        out_shape=jax.ShapeDtypeStruct((depth, b, n), F32),
        compiler_params=_params("arbitrary", "arbitrary"),
        name="adaln",
    )(c, ada_w, ada_b.reshape(depth, 1, n))


def _zero_state(state):
    for ref in state:
        ref[...] = jnp.zeros_like(ref)


class _Mixer(NamedTuple):
    gates: Callable
    body: Callable
    w: jax.Array
    params: tuple
    state_shapes: tuple
    reset: Callable = _zero_state
    pre: Callable | None = None
    row0: int = 0


def _mixer_kernel(x_ref, mod_ref, gain_ref, wg_ref, *rest, mixers, steps_per_seq):
    n = len(mixers)
    pos = 0
    w_refs, params = [], []
    for m in mixers:
        w_refs.append(rest[pos])
        params.append(rest[pos + 1:pos + 1 + len(m.params)])
        pos += 1 + len(m.params)
    o_refs = rest[pos:pos + n]
    g_refs = rest[pos + n:pos + n + 2]
    pos += n + 2
    p_refs, states = [], []
    for m in mixers:
        p_refs.append(rest[pos:pos + 2])
        states.append(rest[pos + 2:pos + 2 + len(m.state_shapes)])
        pos += 2 + len(m.state_shapes)
    s = pl.program_id(0)
    step = x_ref.shape[0]

    @pl.when(s == 0)
    def _():
        for ref in (*g_refs, *[r for pr in p_refs for r in pr], *[r for st in states for r in st]):
            ref[...] = jnp.zeros_like(ref)

    t = (s + steps_per_seq - 1) % steps_per_seq

    @pl.when(t == 0)
    def _():
        for m, st in zip(mixers, states):
            m.reset(st)

    for parity in (0, 1):
        @pl.when(s % 2 == parity)
        def _(parity=parity):
            gate_vals = []
            for i, m in enumerate(mixers):
                p_new, p_cur = p_refs[i][parity], p_refs[i][1 - parity]
                if m.pre is not None:
                    m.pre(p_cur, p_new, t)
                gate_vals.append(m.gates(g_refs[1 - parity], params[i], p_cur, states[i]))
            h = _modulated_norm(x_ref[...], gain_ref[...], mod_ref[0:1, :], mod_ref[1:2, :]).astype(BF16)
            for i, m in enumerate(mixers):
                p_refs[i][parity][m.row0:m.row0 + step, :] = _dot(h, w_refs[i][...])
            g_refs[parity][...] = _dot(h, wg_ref[...])
            for i, m in enumerate(mixers):
                m.body(p_refs[i][1 - parity], gate_vals[i], params[i], o_refs[i], states[i])


def _mixer_call(name, mixers, x2, mod, gain, w_gate, seq, step):
    t, d = x2.shape
    step = min(step, seq)
    n = t // step
    per_seq = seq // step

    def cur(s):
        return jnp.minimum(s, n - 1)

    operands, specs, scratch = [], [], [pltpu.VMEM((step, N_GATE), F32)] * 2
    for m in mixers:
        operands += [m.w, *m.params]
        specs += [pl.BlockSpec(m.w.shape, lambda s: (0, 0), pipeline_mode=pl.Buffered(1)),
                  *[_full_spec(p.shape) for p in m.params]]
        scratch += [pltpu.VMEM((m.row0 + step, m.w.shape[1]), F32)] * 2 + list(m.state_shapes)
    out_spec = pl.BlockSpec((step, d), lambda s: (jnp.maximum(s - 1, 0), 0))
    return pl.pallas_call(
        functools.partial(_mixer_kernel, steps_per_seq=per_seq,
                          mixers=tuple(m._replace(w=None, params=(None,) * len(m.params)) for m in mixers)),
        grid=(n + 1,),
        in_specs=[
            pl.BlockSpec((step, d), lambda s: (cur(s), 0)),
            pl.BlockSpec((None, 6, d), lambda s: (cur(s) // per_seq, 0, 0)),
            pl.BlockSpec((1, d), lambda s: (0, 0)),
            pl.BlockSpec((d, N_GATE), lambda s: (0, 0)),
            *specs,
        ],
        out_specs=[out_spec] * len(mixers),
        out_shape=[jax.ShapeDtypeStruct((t, d), BF16)] * len(mixers),
        scratch_shapes=scratch,
        compiler_params=_params("arbitrary"),
        name=name,
    )(x2, mod, gain, w_gate, *operands)


def _full_spec(shape):
    return pl.BlockSpec(shape, lambda s: (0,) * len(shape))


def _row_refs(c, block, seg, pick):
    parts = []
    for start in range(0, block, seg):
        r = pick(start)
        row = jnp.zeros((1, c.shape[1]), F32) if r is None else c[r:r + 1, :]
        parts.append(jnp.broadcast_to(row, (seg, c.shape[1])))
    return parts[0] if len(parts) == 1 else jnp.concatenate(parts, axis=0)


def _gla_gates(g_ref, params, *_):
    wg_ref, bg_ref, _ = params
    step = g_ref.shape[0]
    block = min(MIX_BLOCK, step)
    row, col = _block_iotas(block)
    tri = (col <= row).astype(BF16)
    w0, w1, _ = _split3(wg_ref[...])
    cum = []
    for b in range(step // block):
        g0, g1, _ = _split3(g_ref[b * block:(b + 1) * block, :])
        x = _dot(g0, w0) + (_dot(g0, w1) + _dot(g1, w0)) + bg_ref[...]
        cum.append(_cumsum_rows(tri, _log_sigmoid(x) * (1.0 / GLA_GATE_NORM)))
    return cum


def _gla_body(p_ref, cum, params, o_ref, state):
    _, _, gain_ref = params
    (st_ref,) = state
    step = o_ref.shape[0]
    block = min(MIX_BLOCK, step)
    off_k, off_v, off_r = GLA_QK, 2 * GLA_QK, 2 * GLA_QK + D_MODEL

    row, col = _block_iotas(block)
    causal = col <= row
    sub_xor = (row // GLA_SUB) ^ (col // GLA_SUB)
    n_levels = (block // GLA_SUB).bit_length() - 1
    gain = gain_ref[...]
    blocks = range(step // block)
    units = [(b, h) for b in blocks for h in range(GLA_HEADS)]

    def rows(b, c0, width):
        return p_ref[b * block:(b + 1) * block, c0:c0 + width]

    c = {u: cum[u[0]][:, u[1] * GLA_DK:(u[1] + 1) * GLA_DK] for u in units}
    q = {u: rows(u[0], u[1] * GLA_DK, GLA_DK) * (GLA_DK ** -0.5) for u in units}
    k = {u: rows(u[0], off_k + u[1] * GLA_DK, GLA_DK) for u in units}
    v = {u: rows(u[0], off_v + u[1] * GLA_DV, GLA_DV).astype(BF16) for u in units}

    ref0 = {u: _row_refs(c[u], block, GLA_SUB, lambda s: None if s == 0 else s - 1) for u in units}
    p = {u: _dot_nt((q[u] * jnp.exp(c[u] - ref0[u])).astype(BF16),
                    (k[u] * jnp.exp(ref0[u] - c[u])).astype(BF16)) for u in units}
    a = {u: jnp.where(causal & (sub_xor == 0), p[u], 0.0) for u in units}
    for lev in range(1, n_levels + 1):
        half = GLA_SUB << (lev - 1)
        sel = causal & (sub_xor >= (1 << (lev - 1))) & (sub_xor < (1 << lev))
        ref = {u: _row_refs(c[u], block, 2 * half, lambda s: s + half - 1) for u in units}
        qs = {u: (q[u] * jnp.exp(jnp.minimum(c[u] - ref[u], 0.0))).astype(BF16) for u in units}
        ks = {u: (k[u] * jnp.exp(jnp.minimum(ref[u] - c[u], 0.0))).astype(BF16) for u in units}
        p = {u: _dot_nt(qs[u], ks[u]) for u in units}
        a = {u: jnp.where(sel, p[u], a[u]) for u in units}
    o_intra = {u: _dot(a[u].astype(BF16), v[u]) for u in units}
    q_dec = {u: (q[u] * jnp.exp(c[u])).astype(BF16) for u in units}
    c_end = {u: c[u][block - 1:block, :] for u in units}
    k_end = {u: (k[u] * jnp.exp(c_end[u] - c[u])).astype(BF16) for u in units}
    kv = {u: _dot_tn(v[u], k_end[u]) for u in units}

    st = [st_ref[h] for h in range(GLA_HEADS)]
    for b in blocks:
        heads = [(b, h) for h in range(GLA_HEADS)]
        o_inter = [_dot_nt(q_dec[u], st[u[1]].astype(BF16)) for u in heads]
        st = [st[u[1]] * jnp.exp(c_end[u]) + kv[u] for u in heads]
        for u, oi in zip(heads, o_inter):
            r = rows(b, off_r + u[1] * GLA_DV, GLA_DV)
            y = _rms(o_intra[u] + oi, gain) * _silu(r)
            o_ref[b * block:(b + 1) * block, u[1] * GLA_DV:(u[1] + 1) * GLA_DV] = y.astype(o_ref.dtype)
    for h in range(GLA_HEADS):
        st_ref[h] = st[h]


def _gla(w, w_g2, b_g2, norm_gain):
    w_g2p = jnp.zeros((N_GATE, GLA_QK), F32).at[LANE_GLR:LANE_GLR + GLA_GATE_RANK].set(w_g2)
    params = (w_g2p, b_g2.reshape(1, GLA_QK), norm_gain.reshape(1, GLA_DV))
    state = (pltpu.VMEM((GLA_HEADS, GLA_DV, GLA_DK), F32),)
    return _Mixer(_gla_gates, _gla_body, w, params, state)


def _mlstm_gates(g_ref, params, *_):
    bias_ref, _ = params
    step = g_ref.shape[0]
    block = min(MIX_BLOCK, step)
    row, col = _block_iotas(block)
    tri = (col <= row).astype(BF16)
    log_i, log_i_t, cum, cum_t = [], [], [], []
    for b in range(step // block):
        capped = ML_GATE_CAP * jnp.tanh((g_ref[b * block:(b + 1) * block, :] + bias_ref[...])
                                        * (1.0 / ML_GATE_CAP))
        cs_b = _cumsum_rows(tri, _log_sigmoid(capped))
        log_i.append(capped)
        log_i_t.append(capped.T)
        cum.append(cs_b)
        cum_t.append(cs_b.T)
    return log_i, log_i_t, cum, cum_t


def _mlstm_body(p_ref, gate_vals, params, o_ref, state):
    _, gain_ref = params
    cs_ref, ns_ref, m_ref = state
    log_i, log_i_t, cum, cum_t = gate_vals
    step = o_ref.shape[0]
    block = min(MIX_BLOCK, step)
    off_k, off_v, off_o = ML_QK, 2 * ML_QK, 2 * ML_QK + D_MODEL

    row, col = _block_iotas(block)
    causal = col <= row
    gain = gain_ref[...]
    blocks = range(step // block)
    heads = range(ML_HEADS)
    units = [(b, h) for b in blocks for h in heads]

    def rows(b, c0, width):
        return p_ref[b * block:(b + 1) * block, c0:c0 + width]

    li_c = {u: log_i[u[0]][:, LANE_ML_I + u[1]:LANE_ML_I + u[1] + 1] for u in units}
    li_r = {u: log_i_t[u[0]][LANE_ML_I + u[1]:LANE_ML_I + u[1] + 1, :] for u in units}
    cum_c = {u: cum[u[0]][:, LANE_ML_F + u[1]:LANE_ML_F + u[1] + 1] for u in units}
    cum_r = {u: cum_t[u[0]][LANE_ML_F + u[1]:LANE_ML_F + u[1] + 1, :] for u in units}
    cum_last = {u: cum_c[u][block - 1:block, :] for u in units}
    log_end = {u: cum_last[u] - cum_c[u] + li_c[u] for u in units}
    m_prev, m_new = {}, {}
    for h in heads:
        m = m_ref[h:h + 1, 0:1]
        for b in blocks:
            u = (b, h)
            m_prev[u] = m
            m = jnp.maximum(cum_last[u] + m, jnp.max(log_end[u], axis=0, keepdims=True))
            m_new[u] = m
        m_ref[h:h + 1, :] = jnp.broadcast_to(m, (1, LANES))

    q = {u: rows(u[0], u[1] * ML_DK, ML_DK) * (ML_DK ** -0.5) for u in units}
    qb = {u: q[u].astype(BF16) for u in units}
    k = {u: rows(u[0], off_k + u[1] * ML_DK, ML_DK) for u in units}
    v = {u: rows(u[0], off_v + u[1] * ML_DV, ML_DV).astype(BF16) for u in units}
    qk = {u: _dot_nt(qb[u], k[u].astype(BF16)) for u in units}
    log_d = {u: jnp.where(causal, cum_c[u] - cum_r[u] + li_r[u], -jnp.inf) for u in units}
    m_inter = {u: cum_c[u] + m_prev[u] for u in units}
    m_t = {u: jnp.maximum(m_inter[u], jnp.max(log_d[u], axis=1, keepdims=True)) for u in units}
    a_inter = {u: jnp.exp(m_inter[u] - m_t[u]) for u in units}
    p = {u: qk[u] * jnp.exp(log_d[u] - m_t[u]) for u in units}
    pv = {u: _dot(p[u].astype(BF16), v[u]) for u in units}
    p_sum = {u: jnp.sum(p[u], axis=1, keepdims=True) for u in units}
    dec = {u: jnp.exp(cum_last[u] + m_prev[u] - m_new[u]) for u in units}
    kw = {u: k[u] * jnp.exp(log_end[u] - m_new[u]) for u in units}
    kwv = {u: _dot_tn(kw[u].astype(BF16), v[u]) for u in units}
    kw_sum = {u: jnp.sum(kw[u], axis=0, keepdims=True) for u in units}

    cs = [cs_ref[h] for h in heads]
    ns = [ns_ref[h:h + 1, :] for h in heads]
    for b in blocks:
        q_cs = [_dot(qb[(b, h)], cs[h].astype(BF16)) for h in heads]
        q_ns = [jnp.sum(q[(b, h)] * ns[h], axis=1, keepdims=True) for h in heads]
        cs = [dec[(b, h)] * cs[h] + kwv[(b, h)] for h in heads]
        ns = [dec[(b, h)] * ns[h] + kw_sum[(b, h)] for h in heads]
        for h in heads:
            u = (b, h)
            num = pv[u] + a_inter[u] * q_cs[h]
            den = p_sum[u] + a_inter[u] * q_ns[h]
            hid = num / jnp.maximum(jnp.abs(den), jnp.exp(-m_t[u]))
            og = rows(b, off_o + h * ML_DV, ML_DV)
            y = _rms(hid, gain) * _sigmoid(og)
            o_ref[b * block:(b + 1) * block, h * ML_DV:(h + 1) * ML_DV] = y.astype(o_ref.dtype)
    for h in heads:
        cs_ref[h] = cs[h]
        ns_ref[h:h + 1, :] = ns[h]


def _mlstm_reset(state):
    cs_ref, ns_ref, m_ref = state
    cs_ref[...] = jnp.zeros_like(cs_ref)
    ns_ref[...] = jnp.zeros_like(ns_ref)
    m_ref[...] = jnp.full_like(m_ref, ML_M_INIT)


def _mlstm(w, i_bias, f_bias, norm_gain):
    bias = jnp.zeros((1, N_GATE), F32)
    bias = bias.at[0, LANE_ML_I:LANE_ML_I + ML_HEADS].set(i_bias)
    bias = bias.at[0, LANE_ML_F:LANE_ML_F + ML_HEADS].set(f_bias)
    params = (bias, norm_gain.reshape(1, ML_DV))
    state = (
        pltpu.VMEM((ML_HEADS, ML_DK, ML_DV), F32),
        pltpu.VMEM((SUBLANES, ML_DK), F32),
        pltpu.VMEM((SUBLANES, LANES), F32),
    )
    return _Mixer(_mlstm_gates, _mlstm_body, w, params, state, reset=_mlstm_reset)


def _gdn_pre(p_cur, p_new, t):
    step = p_cur.shape[0] - SUBLANES
    tail = p_new[step:step + SUBLANES, 0:GDN_QKV]
    p_cur[0:SUBLANES, 0:GDN_QKV] = jnp.where(t == 0, 0.0, tail)


def _gdn_conv(p_ref, conv_ref, qkv_ref, block):
    ext = block + 2 * SUBLANES
    n_delayed = GDN_CONV - 1
    r = lax.broadcasted_iota(jnp.int32, (block, n_delayed * ext), 0)
    c = lax.broadcasted_iota(jnp.int32, (block, n_delayed * ext), 1)
    shift = None
    for j in range(n_delayed):
        hit = c == j * ext + SUBLANES + r - (n_delayed - j)
        shift = hit if shift is None else shift | hit
    shift = shift.astype(BF16)
    for b in range(qkv_ref.shape[0] // block):
        for c0 in range(0, GDN_QKV, CONV_COLS):
            xe = p_ref[b * block:b * block + SUBLANES + block, c0:c0 + CONV_COLS]
            pad = jnp.zeros((SUBLANES, CONV_COLS), F32)
            parts = []
            for j in range(n_delayed):
                parts += [xe * conv_ref[j:j + 1, c0:c0 + CONV_COLS], pad]
            delayed = _dot(shift, jnp.concatenate(parts, axis=0).astype(BF16))
            y = delayed + xe[SUBLANES:, :] * conv_ref[n_delayed:GDN_CONV, c0:c0 + CONV_COLS]
            qkv_ref[b * block:(b + 1) * block, c0:c0 + CONV_COLS] = _silu(y)


def _gdn_gates(g_ref, params, p_ref, state):
    conv_ref, prm_ref, _ = params
    step = g_ref.shape[0]
    block = min(MIX_BLOCK, step)
    _gdn_conv(p_ref, conv_ref, state[1], block)
    row, col = _block_iotas(block)
    tri = (col <= row).astype(BF16)
    cum, cum_t, beta = [], [], []
    for b in range(step // block):
        g = g_ref[b * block:(b + 1) * block, :]
        decay = -jnp.exp(prm_ref[0:1, :]) * _softplus(g + prm_ref[1:2, :])
        cs_b = _cumsum_rows(tri, decay)
        cum.append(cs_b)
        cum_t.append(cs_b.T)
        beta.append(_sigmoid(g))
    return cum, cum_t, beta


def _gdn_body(p_ref, gate_vals, params, o_ref, state):
    _, _, gain_ref = params
    s_ref, qkv_ref = state
    cum, cum_t, beta = gate_vals
    step = o_ref.shape[0]
    block = min(MIX_BLOCK, step)
    off_z = GDN_QKV

    def conv_silu(b, c0):
        return qkv_ref[b * block:(b + 1) * block, c0:c0 + LANES]

    row, col = _block_iotas(block)
    causal = col <= row
    strict = col < row
    eye = (col == row).astype(F32)
    level_masks = [
        ((row >> j == col >> j) & (row >> (j - 1) != col >> (j - 1))).astype(F32)
        for j in range(1, block.bit_length())
    ]
    gain = gain_ref[...]
    blocks = range(step // block)
    heads = range(GDN_HEADS)
    units = [(b, h) for b in blocks for h in heads]

    cum_c = {u: cum[u[0]][:, LANE_GDN_A + u[1]:LANE_GDN_A + u[1] + 1] for u in units}
    cum_r = {u: cum_t[u[0]][LANE_GDN_A + u[1]:LANE_GDN_A + u[1] + 1, :] for u in units}
    beta_c = {u: beta[u[0]][:, LANE_GDN_B + u[1]:LANE_GDN_B + u[1] + 1] for u in units}
    q = {u: conv_silu(u[0], u[1] * GDN_DK) for u in units}
    k = {u: conv_silu(u[0], GDN_HEADS * GDN_DK + u[1] * GDN_DK) for u in units}
    v = {u: conv_silu(u[0], 2 * GDN_HEADS * GDN_DK + u[1] * GDN_DV) for u in units}
    q = {u: q[u] * lax.rsqrt(jnp.sum(q[u] * q[u], axis=-1, keepdims=True) + NORM_EPS) * (GDN_DK ** -0.5)
         for u in units}
    k = {u: k[u] * lax.rsqrt(jnp.sum(k[u] * k[u], axis=-1, keepdims=True) + NORM_EPS) for u in units}
    qb = {u: q[u].astype(BF16) for u in units}
    kb = {u: k[u].astype(BF16) for u in units}
    gamma = {u: jnp.where(causal, jnp.exp(jnp.minimum(cum_c[u] - cum_r[u], 0.0)), 0.0) for u in units}
    kk = {u: _dot_nt(kb[u], kb[u]) for u in units}
    a = {u: jnp.where(strict, beta_c[u] * kk[u] * gamma[u], 0.0) for u in units}
    inv = {u: eye - a[u] * level_masks[0] for u in units}
    for mask in level_masks[1:]:
        xb = {u: inv[u].astype(BF16) for u in units}
        xa = {u: _dot(xb[u], (a[u] * mask).astype(BF16)) for u in units}
        xax = {u: _dot(xa[u].astype(BF16), xb[u]) for u in units}
        inv = {u: inv[u] - xax[u] for u in units}

    e_cum = {u: jnp.exp(cum_c[u]) for u in units}
    rhs = {u: jnp.concatenate([v[u] * beta_c[u], k[u] * (beta_c[u] * e_cum[u])], axis=1).astype(BF16)
           for u in units}
    uw = {u: _dot(inv[u].astype(BF16), rhs[u]) for u in units}
    a_qk = {u: (_dot_nt(qb[u], kb[u]) * gamma[u]).astype(BF16) for u in units}
    q_dec = {u: (q[u] * e_cum[u]).astype(BF16) for u in units}
    cum_last = {u: cum_c[u][block - 1:block, :] for u in units}
    k_end = {u: (k[u] * jnp.exp(cum_last[u] - cum_c[u])).astype(BF16) for u in units}

    st = [s_ref[h] for h in heads]
    for b in blocks:
        us = [(b, h) for h in heads]
        sb = [t_.astype(BF16) for t_ in st]
        ws = [_dot(uw[u][:, GDN_DV:].astype(BF16), sb[u[1]]) for u in us]
        qs = [_dot(q_dec[u], sb[u[1]]) for u in us]
        vb = [(uw[u][:, :GDN_DV] - ws[u[1]]).astype(BF16) for u in us]
        o = [qs[u[1]] + _dot(a_qk[u], vb[u[1]]) for u in us]
        st = [st[u[1]] * jnp.exp(cum_last[u]) + _dot_tn(k_end[u], vb[u[1]]) for u in us]
        for u in us:
            h = u[1]
            z = p_ref[SUBLANES + b * block:SUBLANES + (b + 1) * block, off_z + h * GDN_DV:off_z + (h + 1) * GDN_DV]
            y = _rms(o[h], gain) * _silu(z)
            o_ref[b * block:(b + 1) * block, h * GDN_DV:(h + 1) * GDN_DV] = y.astype(o_ref.dtype)
    for h in heads:
        s_ref[h] = st[h]


def _gdn_reset(state):
    s_ref, _ = state
    s_ref[...] = jnp.zeros_like(s_ref)


def _gdn(w, conv_w, a_log, dt_bias, norm_gain, seq):
    prm = jnp.zeros((SUBLANES, N_GATE), F32)
    prm = prm.at[0, LANE_GDN_A:LANE_GDN_A + GDN_HEADS].set(a_log)
    prm = prm.at[1, LANE_GDN_A:LANE_GDN_A + GDN_HEADS].set(dt_bias)
    params = (conv_w, prm, norm_gain.reshape(1, GDN_DV))
    step = min(GDN_STEP, seq)
    state = (pltpu.VMEM((GDN_HEADS, GDN_DK, GDN_DV), F32), pltpu.VMEM((step, GDN_QKV), F32))
    return _Mixer(_gdn_gates, _gdn_body, w, params, state, reset=_gdn_reset, pre=_gdn_pre, row0=SUBLANES)


def _tail_kernel(yg_ref, ym_ref, yd_ref, x_ref, mod_ref, gmix_ref, gffn_ref, wm_ref, wo_ref, wu_ref,
                 wd_ref, fin_ref, o_ref, *, final_norm):
    x = x_ref[...]
    d = x.shape[1]
    h = _modulated_norm(x, gmix_ref[...], mod_ref[0:1, :], mod_ref[1:2, :]).astype(BF16)
    y = None
    for i, y_ref in enumerate((yg_ref, ym_ref, yd_ref)):
        term = _sigmoid(_dot(h, wm_ref[:, i * d:(i + 1) * d])) * y_ref[...].astype(F32)
        y = term if y is None else y + term
    x = x + mod_ref[2:3, :] * _dot(y.astype(BF16), wo_ref[...])

    h = _modulated_norm(x, gffn_ref[...], mod_ref[3:4, :], mod_ref[4:5, :]).astype(BF16)
    acc = None
    for c in range(D_FF // FF_CHUNK):
        gate = _dot(h, wu_ref[:, c * FF_CHUNK:(c + 1) * FF_CHUNK])
        val = _dot(h, wu_ref[:, D_FF + c * FF_CHUNK:D_FF + (c + 1) * FF_CHUNK])
        part = _dot((_silu(gate) * val).astype(BF16), wd_ref[c * FF_CHUNK:(c + 1) * FF_CHUNK, :])
        acc = part if acc is None else acc + part
    y = x + mod_ref[5:6, :] * acc
    if final_norm:
        y = _rms(y, fin_ref[...])
    o_ref[...] = y


def _tail(y_gla, y_ml, y_gdn, x2, mod, gain_mix, gain_ffn, w_merge, w_out, w_up, w_down, norm_final,
          seq, final_norm):
    t, d = x2.shape
    tm = min(TOK_TM, seq)
    per_seq = seq // tm
    tok = pl.BlockSpec((tm, d), lambda i: (i, 0))

    def resident(shape):
        return pl.BlockSpec(shape, lambda i: (0, 0), pipeline_mode=pl.Buffered(1))

    return pl.pallas_call(
        functools.partial(_tail_kernel, final_norm=final_norm),
        grid=(t // tm,),
        in_specs=[tok, tok, tok, tok,
                  pl.BlockSpec((None, 6, d), lambda i: (i // per_seq, 0, 0)),
                  resident((1, d)), resident((1, d)),
                  resident((d, 3 * d)), resident((d, d)),
                  resident((d, 2 * D_FF)), resident((D_FF, d)),
                  resident((1, d))],
        out_specs=tok,
        out_shape=jax.ShapeDtypeStruct((t, d), F32),
        compiler_params=_params("arbitrary"),
        name="tail",
    )(y_gla, y_ml, y_gdn, x2, mod, gain_mix, gain_ffn, w_merge, w_out, w_up, w_down, norm_final)


def _layout_w_in(w):
    gla_w = 2 * GLA_QK + 2 * D_MODEL
    ml_w = 2 * ML_QK + 2 * D_MODEL
    gdn_w = GDN_QKV + D_MODEL
    o_glr = gla_w
    o_ml = o_glr + GLA_GATE_RANK
    o_mlg = o_ml + ml_w
    o_gdn = o_mlg + 2 * ML_HEADS
    o_gdng = o_gdn + gdn_w
    o_merge = o_gdng + 2 * GDN_HEADS
    small = jnp.concatenate([w[:, o_glr:o_ml], w[:, o_mlg:o_gdn], w[:, o_gdng:o_merge]], axis=1)
    gate = jnp.pad(small, ((0, 0), (0, N_GATE - small.shape[1]))).astype(BF16)
    return (w[:, :gla_w].astype(BF16), w[:, o_ml:o_mlg].astype(BF16), w[:, o_gdn:o_gdng].astype(BF16),
            w[:, o_merge:].astype(BF16), gate)


def kernel(x, c, ada_w, ada_b, norm_mix, norm_ffn, w_in, gla_w_g2, gla_b_g2, gla_norm, ml_i_bias, ml_f_bias, ml_norm, gdn_conv, gdn_a_log, gdn_dt_bias, gdn_norm, w_out, w_ffn_up, w_ffn_down, norm_final):
    b, s, d = x.shape
    depth = ada_w.shape[0]
    mod = _adaln(c, ada_w, ada_b).reshape(depth, b, 6, d)
    x2 = x.reshape(b * s, d)
    fin = norm_final.reshape(1, d)
    for l in range(depth):
        w_gla, w_ml, w_gdn, w_merge, w_gate = _layout_w_in(w_in[l])
        gain = norm_mix[l].reshape(1, d)
        gla = _gla(w_gla, gla_w_g2[l], gla_b_g2[l], gla_norm[l])
        mlstm = _mlstm(w_ml, ml_i_bias[l], ml_f_bias[l], ml_norm[l])
        gdn = _gdn(w_gdn, gdn_conv[l], gdn_a_log[l], gdn_dt_bias[l], gdn_norm[l], s)
        (y_gla,) = _mixer_call("gla", (gla,), x2, mod[l], gain, w_gate, s, GLA_STEP)
        y_gdn, y_ml = _mixer_call("gdn_mlstm", (gdn, mlstm), x2, mod[l], gain, w_gate, s, GDN_STEP)
        x2 = _tail(y_gla, y_ml, y_gdn, x2, mod[l], gain, norm_ffn[l].reshape(1, d), w_merge,
                   w_out[l].astype(BF16), w_ffn_up[l].astype(BF16), w_ffn_down[l].astype(BF16), fin, s,
                   final_norm=(l == depth - 1))
    return x2.reshape(b, s, d)
```
